```python
import jax
import jax.numpy as jnp
from jax import lax
import numpy as np

D_MODEL = 1024
BATCH = 16
SEQ = 256
DEPTH = 4
DEC_BATCH = 8
DEC_SEQ = 2048
PAST_LEN = 512

GRID_W = 64
EPS = 1e-6
NEG_INF = -1e30
BRANCH_W = D_MODEL // 2
N_BRANCH = 3
CONV_K = 3
NA_HEADS = 8
NA_HD = BRANCH_W // NA_HEADS
WIN_R = 8
WIN_C = 16
QCB = 16
KC = 2 * WIN_C
NCB = GRID_W // QCB
ATT_BLOCK = 128
GLA_HEADS = 4
GLA_DV = BRANCH_W // GLA_HEADS
GLA_DK = GLA_DV // 2
GLA_KW = GLA_HEADS * GLA_DK
GLA_RANK = 16
GLA_TAU = 16.0
GLA_CHUNK = 64
ROPE_BASE = 10000.0
SPLIT_SIZES = (BRANCH_W, BRANCH_W, BRANCH_W, BRANCH_W,
               BRANCH_W, BRANCH_W, BRANCH_W, BRANCH_W,
               GLA_KW, GLA_KW, BRANCH_W, BRANCH_W, GLA_RANK, GLA_RANK)
D_IN = sum(SPLIT_SIZES)
SPLIT_IDX = tuple(int(i) for i in np.cumsum(SPLIT_SIZES)[:-1])

kernel_name = 'hybrid_conv_natten_gla_diffusion_step'


def rms_norm(x, g):
    xf = x.astype(jnp.float32)
    y = xf * lax.rsqrt(jnp.mean(xf * xf, axis=-1, keepdims=True) + EPS)
    return (y * g.astype(jnp.float32)).astype(x.dtype)


def to_heads(u, n_heads):
    b, t, _ = u.shape
    return u.reshape(b, t, n_heads, -1).transpose(0, 2, 1, 3)


def from_heads(u):
    b, h, t, d = u.shape
    return u.transpose(0, 2, 1, 3).reshape(b, t, h * d)


def adaln(cond, w_ada, b_ada):
    m = jax.nn.silu(cond) @ w_ada + b_ada
    if m.ndim == 2:
        m = m[:, None, :]
    return jnp.split(m, 3, axis=-1)


def short_conv(u, w):
    t = u.shape[1]
    pad = CONV_K // 2
    up = jnp.pad(u, ((0, 0), (pad, pad), (0, 0)))
    y = up[:, 0:t] * w[0]
    for i in range(1, CONV_K):
        y = y + up[:, i:i + t] * w[i]
    return y


def rope_2d(x):
    t = x.shape[2]
    pos = np.arange(t)
    n_f = GLA_DK // 4
    inv = ROPE_BASE ** (-np.arange(n_f) / n_f)
    ang_r = jnp.asarray((pos // GRID_W)[:, None] * inv, jnp.float32)
    ang_c = jnp.asarray((pos % GRID_W)[:, None] * inv, jnp.float32)

    def rot(u, ang):
        a, b = jnp.split(u, 2, axis=-1)
        cos, sin = jnp.cos(ang), jnp.sin(ang)
        return jnp.concatenate([a * cos - b * sin, a * sin + b * cos], axis=-1)

    xf = x.astype(jnp.float32)
    half = GLA_DK // 2
    return jnp.concatenate([rot(xf[..., :half], ang_r), rot(xf[..., half:], ang_c)], axis=-1).astype(x.dtype)


def na_context(q, k, v):
    b, h, t, d = q.shape
    nb = t // ATT_BLOCK
    qb = q.reshape(b, h, nb, ATT_BLOCK, d).transpose(2, 0, 1, 3, 4)

    def block(qi):
        s = jnp.einsum('bhqd,bhkd->bhqk', qi, k).astype(jnp.float32) * NA_HD ** -0.5
        p = jax.nn.softmax(s, axis=-1).astype(v.dtype)
        return jnp.einsum('bhqk,bhkd->bhqd', p, v)

    o = lax.map(block, qb)
    return o.transpose(1, 2, 0, 3, 4).reshape(b, h, t, d)


def na_latent(q, k, v, k_ctx, v_ctx, rpb):
    b, h, t, d = q.shape
    rows = t // GRID_W
    kr = min(WIN_R, rows)
    r = np.arange(rows)
    rs = np.clip(r - kr // 2, 0, rows - kr)
    row_idx = rs[:, None] + np.arange(kr)
    dr = row_idx - r[:, None]
    kcs = np.clip(np.arange(NCB) * QCB - WIN_C // 2, 0, GRID_W - KC)
    col_idx = kcs[:, None] + np.arange(KC)
    qcol = np.arange(GRID_W).reshape(NCB, QCB)
    cs = np.clip(qcol - WIN_C // 2, 0, GRID_W - WIN_C)
    kcol = col_idx[:, None, :]
    mask = (kcol >= cs[..., None]) & (kcol < cs[..., None] + WIN_C)
    dc = np.clip(kcol - qcol[..., None] + WIN_C - 1, 0, 2 * WIN_C - 2)
    key_idx = (row_idx[:, None, :, None] * GRID_W + col_idx[None, :, None, :]).reshape(rows, NCB, kr * KC)
    nw = kr * KC

    bias = rpb[:, (dr + WIN_R - 1)[:, None, None, :, None], dc[None, :, :, None, :]].astype(jnp.float32)
    bias = jnp.where(mask[None, None, :, :, None, :], bias, NEG_INF).reshape(h, rows, NCB, QCB, nw)

    qg = q.reshape(b, h, rows, NCB, QCB, d)
    kw = jnp.take(k, key_idx, axis=2)
    vw = jnp.take(v, key_idx, axis=2)
    scale = NA_HD ** -0.5
    s_win = jnp.einsum('bhrjid,bhrjmd->bhrjim', qg, kw).astype(jnp.float32) * scale + bias
    s_ctx = jnp.einsum('bhrjid,bhnd->bhrjin', qg, k_ctx).astype(jnp.float32) * scale
    p = jax.nn.softmax(jnp.concatenate([s_win, s_ctx], axis=-1), axis=-1).astype(v.dtype)
    o = (jnp.einsum('bhrjim,bhrjmd->bhrjid', p[..., :nw], vw)
         + jnp.einsum('bhrjin,bhnd->bhrjid', p[..., nw:], v_ctx))
    return o.reshape(b, h, t, d)


def gla_chunked(q, k, v, log_a, s0):
    b, h, t, dk = q.shape
    dv = v.shape[-1]
    n = t // GLA_CHUNK

    def chunks(u):
        return u.astype(jnp.float32).reshape(b, h, n, GLA_CHUNK, u.shape[-1]).transpose(2, 0, 1, 3, 4)

    lower = jnp.tril(jnp.ones((GLA_CHUNK, GLA_CHUNK), dtype=bool))[:, :, None]

    def step(s, inp):
        qc, kc, vc, lc = inp
        bc = jnp.cumsum(lc, axis=2)
        o_inter = jnp.einsum('bhtk,bhkv->bhtv', qc * jnp.exp(bc), s)
        diff = bc[:, :, :, None, :] - bc[:, :, None, :, :]
        decay = jnp.where(lower, jnp.exp(jnp.where(lower, diff, 0.0)), 0.0)
        att = jnp.einsum('bhtk,bhsk,bhtsk->bhts', qc, kc, decay)
        o = o_inter + jnp.einsum('bhts,bhsv->bhtv', att, vc)
        b_last = bc[:, :, -1, :]
        s_new = (jnp.exp(b_last)[..., None] * s
                 + jnp.einsum('bhsk,bhsv->bhkv', kc * jnp.exp(b_last[:, :, None, :] - bc), vc))
        return s_new, o

    s_fin, o = lax.scan(step, s0.astype(jnp.float32), (chunks(q), chunks(k), chunks(v), chunks(log_a)))
    return o.transpose(1, 2, 0, 3, 4).reshape(b, h, t, dv), s_fin


def gla_bidir(q, k, v, la_f, la_b, s_f0, s_b0):
    o_f, s_f = gla_chunked(q, k, v, la_f, s_f0)
    flip = lambda u: jnp.flip(u, axis=2)
    o_b, s_b = gla_chunked(flip(q), flip(k), flip(v), flip(la_b), s_b0)
    return o_f + flip(o_b), s_f, s_b


def log_decay(lr, w, b):
    z = (lr @ w + b).astype(jnp.float32)
    return to_heads(jax.nn.log_sigmoid(z) / GLA_TAU, GLA_HEADS)


def trunk_layer(x, mod, ctx, P):
    shift, scale, gate = mod
    h = rms_norm(x, P['norm_w']) * (1.0 + scale) + shift
    (xa, ba, ca, ga, qn, kn, vn, gn, qg, kg, vg, gg, lrf, lrb) = jnp.split(h @ P['w_in'], SPLIT_IDX, axis=-1)

    y_conv = ba * short_conv(ca * xa, P['conv_w']) * jax.nn.silu(ga)

    q = rms_norm(to_heads(qn, NA_HEADS), P['q_norm_w'])
    k = rms_norm(to_heads(kn, NA_HEADS), P['k_norm_w'])
    v = to_heads(vn, NA_HEADS)

    qc = to_heads(qg, GLA_HEADS)
    kc = to_heads(kg, GLA_HEADS)
    vc = to_heads(vg, GLA_HEADS)
    la_f = log_decay(lrf, P['w_alpha'][0], P['b_alpha'][0])
    la_b = log_decay(lrb, P['w_alpha'][1], P['b_alpha'][1])

    if ctx is None:
        o_na = na_context(q, k, v)
        zeros = jnp.zeros((x.shape[0], GLA_HEADS, GLA_DK, GLA_DV), jnp.float32)
        s_f0, s_b0 = zeros, zeros
    else:
        k_ctx, v_ctx, s_f0, s_b0 = ctx
        o_na = na_latent(q, k, v, k_ctx, v_ctx, P['rpb'])
        qc = rope_2d(qc)
        kc = rope_2d(kc)

    o_gla, s_f, s_b = gla_bidir(qc * GLA_DK ** -0.5, kc, vc, la_f, la_b, s_f0, s_b0)

    y_na = from_heads(o_na) * jax.nn.silu(gn)
    y_gla = from_heads(rms_norm(o_gla, P['gla_norm_w'])).astype(x.dtype) * jax.nn.silu(gg)

    g_conv, g_na, g_gla = jnp.split(jax.nn.sigmoid(h @ P['w_gate'] + P['b_gate']), N_BRANCH, axis=-1)
    wb = P['w_branch']
    merged = g_conv * (y_conv @ wb[0]) + g_na * (y_na @ wb[1]) + g_gla * (y_gla @ wb[2])
    x_new = x + gate * (merged @ P['w_out'])
    return x_new, (k, v, s_f, s_b)


def setup_inputs(seed: int = 0) -> dict:
    key = jax.random.key(seed)
    ks = jax.random.split(key, 24)
    f32 = jnp.float32

    def nrm(k, shape, s):
        return jax.random.normal(k, shape, f32) * s

    return {
        'x_prompt': nrm(ks[0], (BATCH, SEQ, D_MODEL), 1.0),
        'x_sample': nrm(ks[1], (DEC_BATCH, DEC_SEQ, D_MODEL), 1.0),
        'c': nrm(ks[2], (DEC_BATCH, D_MODEL), 1.0),
        'cache_k': nrm(ks[3], (DEC_BATCH, DEPTH, NA_HEADS, PAST_LEN, NA_HD), 1.0),
        'cache_v': nrm(ks[4], (DEC_BATCH, DEPTH, NA_HEADS, PAST_LEN, NA_HD), 1.0),
        'state_fwd': nrm(ks[5], (DEC_BATCH, DEPTH, GLA_HEADS, GLA_DK, GLA_DV), 0.5),
        'state_bwd': nrm(ks[6], (DEC_BATCH, DEPTH, GLA_HEADS, GLA_DK, GLA_DV), 0.5),
        'c_ctx': nrm(ks[7], (D_MODEL,), 1.0),
        'norm_w': 1.0 + nrm(ks[8], (DEPTH, D_MODEL), 0.01),
        'w_ada': nrm(ks[9], (DEPTH, D_MODEL, 3 * D_MODEL), D_MODEL ** -0.5),
        'b_ada': nrm(ks[10], (DEPTH, 3 * D_MODEL), 0.02),
        'w_in': nrm(ks[11], (DEPTH, D_MODEL, D_IN), D_MODEL ** -0.5),
        'conv_w': nrm(ks[12], (DEPTH, CONV_K, BRANCH_W), CONV_K ** -0.5),
        'q_norm_w': 1.0 + nrm(ks[13], (DEPTH, NA_HD), 0.01),
        'k_norm_w': 1.0 + nrm(ks[14], (DEPTH, NA_HD), 0.01),
        'rpb': nrm(ks[15], (DEPTH, NA_HEADS, 2 * WIN_R - 1, 2 * WIN_C - 1), 0.1),
        'w_alpha': nrm(ks[16], (DEPTH, 2, GLA_RANK, GLA_KW), GLA_RANK ** -0.5),
        'b_alpha': nrm(ks[17], (DEPTH, 2, GLA_KW), 0.1),
        'gla_norm_w': 1.0 + nrm(ks[18], (DEPTH, GLA_DV), 0.01),
        'w_branch': nrm(ks[19], (DEPTH, N_BRANCH, BRANCH_W, D_MODEL), BRANCH_W ** -0.5),
        'w_gate': nrm(ks[20], (DEPTH, D_MODEL, N_BRANCH * D_MODEL), D_MODEL ** -0.5),
        'b_gate': nrm(ks[21], (DEPTH, N_BRANCH * D_MODEL), 0.02),
        'w_out': nrm(ks[22], (DEPTH, D_MODEL, D_MODEL), D_MODEL ** -0.5),
    }


def reference(x_prompt, x_sample, c, cache_k, cache_v, state_fwd, state_bwd, c_ctx,
              norm_w, w_ada, b_ada, w_in, conv_w, q_norm_w, k_norm_w, rpb,
              w_alpha, b_alpha, gla_norm_w, w_branch, w_gate, b_gate, w_out):
    y_p = x_prompt
    y_s = x_sample
    ks, vs, sfs, sbs = [], [], [], []
    for l in range(DEPTH):
        P = {
            'norm_w': norm_w[l], 'w_in': w_in[l], 'conv_w': conv_w[l],
            'q_norm_w': q_norm_w[l], 'k_norm_w': k_norm_w[l], 'rpb': rpb[l],
            'w_alpha': w_alpha[l], 'b_alpha': b_alpha[l], 'gla_norm_w': gla_norm_w[l],
            'w_branch': w_branch[l], 'w_gate': w_gate[l], 'b_gate': b_gate[l], 'w_out': w_out[l],
        }
        mod_ctx = adaln(c_ctx, w_ada[l], b_ada[l])
        y_p, (k_l, v_l, sf_l, sb_l) = trunk_layer(y_p, mod_ctx, None, P)
        ks.append(k_l)
        vs.append(v_l)
        sfs.append(sf_l)
        sbs.append(sb_l)
        mod_lat = adaln(c, w_ada[l], b_ada[l])
        ctx = (cache_k[:, l], cache_v[:, l], state_fwd[:, l], state_bwd[:, l])
        y_s, _ = trunk_layer(y_s, mod_lat, ctx, P)
    return (y_p, y_s, jnp.stack(ks, axis=1), jnp.stack(vs, axis=1), jnp.stack(sfs, axis=1), jnp.stack(sbs, axis=1))
```

```python
import functools

import numpy as np
import jax
import jax.numpy as jnp
from jax import lax
from jax.experimental import pallas as pl
from jax.experimental.pallas import tpu as pltpu

F32 = jnp.float32
BF16 = jnp.bfloat16

D_MODEL = 1024
DEPTH = 4
GRID_W = 64
EPS = 1e-6
NEG_INF = -1e30
BRANCH_W = D_MODEL // 2
N_BRANCH = 3
NA_HEADS = 8
NA_HD = BRANCH_W // NA_HEADS
WIN_R = 8
WIN_C = 16
GLA_HEADS = 4
GLA_DV = BRANCH_W // GLA_HEADS
GLA_DK = GLA_DV // 2
GLA_KW = GLA_HEADS * GLA_DK
GLA_RANK = 16
GLA_TAU = 16.0
ROPE_BASE = 10000.0

LANES = 128
D_IN = 8 * BRANCH_W + 2 * GLA_KW + 2 * BRANCH_W + 2 * GLA_RANK
D_IN_PAD = ((D_IN + LANES - 1) // LANES) * LANES
C_XA, C_BA, C_CA, C_GA = 0, 512, 1024, 1536
C_QN, C_KN, C_VN, C_GN = 2048, 2560, 3072, 3584
C_QG, C_KG, C_VG, C_GG, C_LR = 4096, 4352, 4608, 5120, 5632

VMEM_LIMIT = 56 * 1024 * 1024
TOKEN_TILE = 512
GLA_CHUNK = 64
NA_QROWS = 4
NA_KROWS = 12
HALO = 16


def _cparams(n_axes):
    return pltpu.CompilerParams(dimension_semantics=("arbitrary",) * n_axes,
                                vmem_limit_bytes=VMEM_LIMIT)


def _sigmoid(x):
    return 1.0 / (1.0 + jnp.exp(-x))


def _silu(x):
    return x * _sigmoid(x)


def _log_sigmoid(x):
    return jnp.minimum(x, 0.0) - jnp.log(1.0 + jnp.exp(-jnp.abs(x)))


def _dot(a, b):
    return jnp.dot(a, b, preferred_element_type=F32)


def _dot_nt(a, b):
    return lax.dot_general(a, b, (((1,), (1,)), ((), ())), preferred_element_type=F32)


def _dot_tn(a, b):
    return lax.dot_general(a, b, (((0,), (0,)), ((), ())), preferred_element_type=F32)


def _ada_kernel(cond_ref, w_ref, b_ref, o_ref):
    c = cond_ref[...]
    o_ref[0] = _dot(_silu(c).astype(BF16), w_ref[0].astype(BF16)) + b_ref[0]


def _adaln(cond, w_ada, b_ada):
    r = cond.shape[0]
    tn = 768
    return pl.pallas_call(
        _ada_kernel,
        grid=(DEPTH, 3 * D_MODEL // tn),
        in_specs=[
            pl.BlockSpec((r, D_MODEL), lambda l, n: (0, 0)),
            pl.BlockSpec((1, D_MODEL, tn), lambda l, n: (l, 0, n)),
            pl.BlockSpec((1, 1, tn), lambda l, n: (l, 0, n)),
        ],
        out_specs=pl.BlockSpec((1, r, tn), lambda l, n: (l, 0, n)),
        out_shape=jax.ShapeDtypeStruct((DEPTH, r, 3 * D_MODEL), F32),
        compiler_params=_cparams(2),
        name="adaln",
    )(cond, w_ada, b_ada.reshape(DEPTH, 1, 3 * D_MODEL))


def _modulated_norm(x, mod_ref, nw_ref):
    ms = jnp.mean(x * x, axis=-1, keepdims=True)
    xn = x * lax.rsqrt(ms + EPS) * nw_ref[...]
    shift = mod_ref[0, :, 0:D_MODEL]
    scale = mod_ref[0, :, D_MODEL:2 * D_MODEL]
    return xn * (1.0 + scale) + shift


def _inproj_kernel(*refs, rope):
    if rope:
        (x_ref, mod_ref, nw_ref, w_ref, hm_ref, qw_ref, kw_ref, rc_ref, rs1_ref, rs2_ref,
         pc_ref, gc_ref, q_ref, k_ref, v_ref, gn_ref, qg_ref, kg_ref, vg_ref, gg_ref, lr_ref) = refs
    else:
        (x_ref, mod_ref, nw_ref, w_ref, hm_ref, qw_ref, kw_ref,
         pc_ref, gc_ref, q_ref, k_ref, v_ref, gn_ref, qg_ref, kg_ref, vg_ref, gg_ref, lr_ref) = refs
    h = _modulated_norm(x_ref[...], mod_ref, nw_ref).astype(BF16)

    def proj(c0, n):
        return _dot(h, w_ref[:, c0:c0 + n])

    pc_ref[...] = (proj(C_CA, BRANCH_W) * proj(C_XA, BRANCH_W)).astype(pc_ref.dtype)
    gc_ref[...] = (proj(C_BA, BRANCH_W) * _silu(proj(C_GA, BRANCH_W))).astype(gc_ref.dtype)

    def head_norm(u, g_ref):
        ms = _dot((u * u).astype(BF16), hm_ref[...])
        return u * lax.rsqrt(ms + EPS) * g_ref[...]

    q_ref[...] = (head_norm(proj(C_QN, BRANCH_W), qw_ref) * (NA_HD ** -0.5)).astype(q_ref.dtype)
    k_ref[...] = head_norm(proj(C_KN, BRANCH_W), kw_ref).astype(k_ref.dtype)
    v_ref[...] = proj(C_VN, BRANCH_W).astype(v_ref.dtype)
    gn_ref[...] = _silu(proj(C_GN, BRANCH_W)).astype(gn_ref.dtype)

    qg = proj(C_QG, GLA_KW) * (GLA_DK ** -0.5)
    kg = proj(C_KG, GLA_KW)
    if rope:
        def rot(u):
            parts = []
            for j in range(GLA_KW // LANES):
                uj = u[:, j * LANES:(j + 1) * LANES]
                parts.append(uj * rc_ref[...]
                             + pltpu.roll(uj, LANES - GLA_DK // 4, axis=1) * rs1_ref[...]
                             + pltpu.roll(uj, GLA_DK // 4, axis=1) * rs2_ref[...])
            return jnp.concatenate(parts, axis=1)
        qg = rot(qg)
        kg = rot(kg)
    qg_ref[...] = qg.astype(qg_ref.dtype)
    kg_ref[...] = kg.astype(kg_ref.dtype)
    vg_ref[...] = proj(C_VG, BRANCH_W).astype(vg_ref.dtype)
    gg_ref[...] = _silu(proj(C_GG, BRANCH_W)).astype(gg_ref.dtype)
    lr_ref[...] = proj(C_LR, LANES)


def _inproj(x, mod, nw, w_in, hm, qw, kw, rope_tabs, seq, kv_dtype):
    n_tok = x.shape[0]
    tm = min(TOKEN_TILE, seq)
    tiles_per_seq = seq // tm
    rope = rope_tabs is not None

    def tok(w):
        return pl.BlockSpec((tm, w), lambda i: (i, 0))

    def const(shape):
        return pl.BlockSpec(shape, lambda i: (0,) * len(shape))

    in_specs = [
        tok(D_MODEL),
        pl.BlockSpec((1, 1, 3 * D_MODEL), lambda i: (i // tiles_per_seq, 0, 0)),
        const((1, D_MODEL)),
        const((D_MODEL, D_IN_PAD)),
        const((BRANCH_W, BRANCH_W)),
        const((1, BRANCH_W)),
        const((1, BRANCH_W)),
    ]
    args = [x, mod, nw, w_in, hm, qw, kw]
    if rope:
        in_specs += [pl.BlockSpec((tm, LANES), lambda i: (i % tiles_per_seq, 0))] * 3
        args += list(rope_tabs)
    widths = [BRANCH_W, BRANCH_W, BRANCH_W, BRANCH_W, BRANCH_W, BRANCH_W,
              GLA_KW, GLA_KW, BRANCH_W, BRANCH_W, LANES]
    dtypes = [BF16, BF16, BF16, kv_dtype, kv_dtype, BF16, BF16, BF16, BF16, BF16, F32]
    return pl.pallas_call(
        functools.partial(_inproj_kernel, rope=rope),
        grid=(n_tok // tm,),
        in_specs=in_specs,
        out_specs=[tok(w) for w in widths],
        out_shape=[jax.ShapeDtypeStruct((n_tok, w), d) for w, d in zip(widths, dtypes)],
        compiler_params=_cparams(1),
        name="inproj_rope" if rope else "inproj",
    )(*args)


def _pair_masks(dtype):
    lane = lax.broadcasted_iota(jnp.int32, (1, LANES), 1)
    return lane < NA_HD, lane >= NA_HD


def _softmax_pv(scores, values):
    m = functools.reduce(jnp.maximum, [jnp.max(s, axis=-1, keepdims=True) for s in scores])
    ps = [jnp.exp(s - m) for s in scores]
    l = functools.reduce(lambda a, b: a + b, [jnp.sum(p, axis=-1, keepdims=True) for p in ps])
    o = functools.reduce(lambda a, b: a + b,
                         [_dot(p.astype(BF16), v) for p, v in zip(ps, values)])
    return o / l


def _na_ctx_kernel(q_ref, k_ref, v_ref, gn_ref, o_ref):
    m0, m1 = _pair_masks(BF16)
    for hp in range(NA_HEADS // 2):
        cols = slice(hp * LANES, (hp + 1) * LANES)
        qp = q_ref[0, :, cols]
        kp = k_ref[0, :, cols].astype(BF16)
        vp = v_ref[0, :, cols].astype(BF16)
        outs = []
        for msk in (m0, m1):
            qa = jnp.where(msk, qp, jnp.zeros_like(qp))
            outs.append(_softmax_pv([_dot_nt(qa, kp)], [vp]))
        o = jnp.where(m0, outs[0], outs[1])
        o_ref[0, :, cols] = (o * gn_ref[0, :, cols].astype(F32)).astype(o_ref.dtype)


def _na_ctx(q, k, v, gn, batch, seq):
    def blk():
        return pl.BlockSpec((1, seq, BRANCH_W), lambda b: (b, 0, 0))
    r3 = lambda u: u.reshape(batch, seq, BRANCH_W)
    out = pl.pallas_call(
        _na_ctx_kernel,
        grid=(batch,),
        in_specs=[blk(), blk(), blk(), blk()],
        out_specs=blk(),
        out_shape=jax.ShapeDtypeStruct((batch, seq, BRANCH_W), BF16),
        compiler_params=_cparams(1),
        name="na_ctx",
    )(r3(q), r3(k), r3(v), r3(gn))
    return out.reshape(batch * seq, BRANCH_W)


def _na_window_start(j, rows):
    return jnp.clip(j * NA_QROWS - WIN_R // 2, 0, rows - NA_KROWS)


def _na_lat_kernel(q_ref, k_ref, v_ref, kc_ref, vc_ref, bias_ref, gn_ref, o_ref, *, rows):
    j = pl.program_id(2)
    start = pl.multiple_of(_na_window_start(j, rows) * GRID_W, NA_QROWS * GRID_W)
    m0, m1 = _pair_masks(BF16)
    qp = q_ref[0]
    kw = k_ref[0, pl.ds(start, NA_KROWS * GRID_W), :]
    vw = v_ref[0, pl.ds(start, NA_KROWS * GRID_W), :]
    kc = kc_ref[0]
    vc = vc_ref[0]
    outs = []
    for a, msk in enumerate((m0, m1)):
        qa = jnp.where(msk, qp, jnp.zeros_like(qp))
        s_w = _dot_nt(qa, kw) + bias_ref[0, a].astype(F32)
        s_c = _dot_nt(qa, kc)
        outs.append(_softmax_pv([s_w, s_c], [vw, vc]))
    o = jnp.where(m0, outs[0], outs[1])
    o_ref[0] = (o * gn_ref[0].astype(F32)).astype(o_ref.dtype)


def _na_lat(q, k, v, kc, vc, bias, gn, batch, seq):
    rows = seq // GRID_W
    nj = rows // NA_QROWS
    tq = NA_QROWS * GRID_W

    def bias_type(j):
        return jnp.where(j == 0, 0, jnp.where(j == nj - 1, 2, 1))

    qblk = pl.BlockSpec((1, tq, LANES), lambda b, hp, j: (b, j, hp))
    seqblk = pl.BlockSpec((1, seq, LANES), lambda b, hp, j: (b, 0, hp))
    ctxblk = pl.BlockSpec((1, kc.shape[1], LANES), lambda b, hp, j: (b, 0, hp))
    r3 = lambda u: u.reshape(batch, seq, BRANCH_W)
    out = pl.pallas_call(
        functools.partial(_na_lat_kernel, rows=rows),
        grid=(batch, NA_HEADS // 2, nj),
        in_specs=[qblk, seqblk, seqblk, ctxblk, ctxblk,
                  pl.BlockSpec((1, 2, tq, NA_KROWS * GRID_W),
                               lambda b, hp, j: (hp * 3 + bias_type(j), 0, 0, 0)),
                  qblk],
        out_specs=qblk,
        out_shape=jax.ShapeDtypeStruct((batch, seq, BRANCH_W), BF16),
        compiler_params=_cparams(3),
        name="na_lat",
    )(r3(q), r3(k), r3(v), kc, vc, bias, r3(gn))
    return out.reshape(batch * seq, BRANCH_W)


def _na_bias_table(rpb, rows):
    nj = rows // NA_QROWS
    tabs = []
    for j in (0, 1, nj - 1):
        start = int(np.clip(j * NA_QROWS - WIN_R // 2, 0, rows - NA_KROWS))
        r = j * NA_QROWS + np.arange(NA_QROWS)
        rs = np.clip(r - WIN_R // 2, 0, rows - WIN_R)
        krow = start + np.arange(NA_KROWS)
        dr = krow[None, :] - r[:, None]
        row_ok = (krow[None, :] >= rs[:, None]) & (krow[None, :] < rs[:, None] + WIN_R)
        qc = np.arange(GRID_W)
        cs = np.clip(qc - WIN_C // 2, 0, GRID_W - WIN_C)
        kcol = np.arange(GRID_W)
        dc = kcol[None, :] - qc[:, None]
        col_ok = (kcol[None, :] >= cs[:, None]) & (kcol[None, :] < cs[:, None] + WIN_C)
        ok = row_ok[:, None, :, None] & col_ok[None, :, None, :]
        dri = np.clip(dr + WIN_R - 1, 0, 2 * WIN_R - 2)
        dci = np.clip(dc + WIN_C - 1, 0, 2 * WIN_C - 2)
        b = rpb[:, dri[:, None, :, None], dci[None, :, None, :]]
        b = jnp.where(ok[None], b, NEG_INF)
        tabs.append(b.reshape(NA_HEADS // 2, 2, NA_QROWS * GRID_W, NA_KROWS * GRID_W))
    t = jnp.stack(tabs, axis=1)
    return t.reshape(NA_HEADS // 2 * 3, 2, NA_QROWS * GRID_W, NA_KROWS * GRID_W).astype(BF16)


def _gla_kernel(q_ref, k_ref, v_ref, gg_ref, lr_ref, wa_ref, ba_ref, gw_ref, bd_ref,
                s0f_ref, s0b_ref, y_ref, sf_ref, sb_ref, of_ref, ob_ref, stf_ref, stb_ref, *, seq):
    c = GLA_CHUNK
    n = seq // c
    row = lax.broadcasted_iota(jnp.int32, (c, c), 0)
    col = lax.broadcasted_iota(jnp.int32, (c, c), 1)
    lower = row >= col
    tri_f = jnp.where(lower, 1.0, 0.0).astype(BF16)
    tri_b = jnp.where(row <= col, 1.0, 0.0).astype(BF16)
    lane = lax.broadcasted_iota(jnp.int32, (1, GLA_KW), 1)
    head_masks = [(lane >= h * GLA_DK) & (lane < (h + 1) * GLA_DK) for h in range(GLA_HEADS)]
    wa = wa_ref[...].astype(BF16)

    stf_ref[...] = s0f_ref[0]
    stb_ref[...] = s0b_ref[0]

    def chunk(r0, la_cols, tri, causal, ref_row, end_row, st_ref, out_ref):
        sl = pl.ds(r0, c)
        z = _dot(lr_ref[0, sl, :].astype(BF16), wa[:, la_cols]) + ba_ref[:, la_cols]
        la = _log_sigmoid(z) * (1.0 / GLA_TAU)
        la_hi = la.astype(BF16)
        la_lo = (la - la_hi.astype(F32)).astype(BF16)
        bc = _dot(tri, la_hi) + _dot(tri, la_lo)
        ref = bc[ref_row:ref_row + 1, :]
        end = bc[end_row:end_row + 1, :]
        qd = q_ref[0, sl, :].astype(F32) * jnp.exp(bc - ref)
        kd = k_ref[0, sl, :].astype(F32) * jnp.exp(ref - bc)
        kd_b = kd.astype(BF16)
        vc = v_ref[0, sl, :]
        st = st_ref[...]
        o = _dot_nt((qd * jnp.exp(ref)).astype(BF16), st.astype(BF16))
        qd_b = qd.astype(BF16)
        parts = []
        for h in range(GLA_HEADS):
            qh = jnp.where(head_masks[h], qd_b, jnp.zeros_like(qd_b))
            att = jnp.where(causal, _dot_nt(qh, kd_b), 0.0)
            parts.append(_dot(att.astype(BF16), vc[:, h * GLA_DV:(h + 1) * GLA_DV]))
        out_ref[sl, :] = o + jnp.concatenate(parts, axis=1)
        upd = _dot_tn(vc, (kd * jnp.exp(end - ref)).astype(BF16))
        st_ref[...] = st * jnp.exp(end) + upd * bd_ref[...]

    def body(i, carry):
        rf = pl.multiple_of(i * c, c)
        rb = pl.multiple_of((n - 1 - i) * c, c)
        chunk(rf, slice(0, GLA_KW), tri_f, lower, c // 2 - 1, c - 1, stf_ref, of_ref)
        chunk(rb, slice(GLA_KW, 2 * GLA_KW), tri_b, row <= col, c // 2, 0, stb_ref, ob_ref)
        return carry

    lax.fori_loop(0, n, body, 0)
    sf_ref[0] = stf_ref[...]
    sb_ref[0] = stb_ref[...]

    def finish(i, carry):
        sl = pl.ds(pl.multiple_of(i * c, c), c)
        o = of_ref[sl, :] + ob_ref[sl, :]
        parts = []
        for h in range(GLA_HEADS):
            oh = o[:, h * GLA_DV:(h + 1) * GLA_DV]
            ms = jnp.mean(oh * oh, axis=-1, keepdims=True)
            parts.append(oh * lax.rsqrt(ms + EPS))
        y = jnp.concatenate(parts, axis=1) * gw_ref[...]
        y_ref[0, sl, :] = (y * gg_ref[0, sl, :].astype(F32)).astype(y_ref.dtype)
        return carry

    lax.fori_loop(0, n, finish, 0)


def _gla(qg, kg, vg, gg, lr, wa, ba, gw, bd, s0f, s0b, batch, seq):
    hv, hk = GLA_HEADS * GLA_DV, GLA_KW

    def seqblk(w):
        return pl.BlockSpec((1, seq, w), lambda b: (b, 0, 0))

    def const(shape):
        return pl.BlockSpec(shape, lambda b: (0,) * len(shape))

    stblk = pl.BlockSpec((1, hv, hk), lambda b: (b, 0, 0))
    r3 = lambda u: u.reshape(batch, seq, u.shape[-1])
    y, sf, sb = pl.pallas_call(
        functools.partial(_gla_kernel, seq=seq),
        grid=(batch,),
        in_specs=[seqblk(hk), seqblk(hk), seqblk(hv), seqblk(hv), seqblk(LANES),
                  const((LANES, 2 * hk)), const((1, 2 * hk)), const((1, hv)), const((hv, hk)),
                  stblk, stblk],
        out_specs=[seqblk(hv), stblk, stblk],
        out_shape=[jax.ShapeDtypeStruct((batch, seq, hv), BF16),
                   jax.ShapeDtypeStruct((batch, hv, hk), F32),
                   jax.ShapeDtypeStruct((batch, hv, hk), F32)],
        scratch_shapes=[pltpu.VMEM((seq, hv), F32), pltpu.VMEM((seq, hv), F32),
                        pltpu.VMEM((hv, hk), F32), pltpu.VMEM((hv, hk), F32)],
        compiler_params=_cparams(1),
        name="gla",
    )(r3(qg), r3(kg), r3(vg), r3(gg), r3(lr), wa, ba, gw, bd, s0f, s0b)
    return y.reshape(batch * seq, hv), sf, sb


def _state_to_blockdiag_t(s):
    b = s.shape[0]
    eye = jnp.eye(GLA_HEADS, dtype=s.dtype)
    t = jnp.einsum("bhkv,hg->bhvgk", s, eye)
    return t.reshape(b, GLA_HEADS * GLA_DV, GLA_KW)


def _blockdiag_t_to_state(t):
    b = t.shape[0]
    t5 = t.reshape(b, GLA_HEADS, GLA_DV, GLA_HEADS, GLA_DK)
    d = jnp.stack([t5[:, h, :, h, :] for h in range(GLA_HEADS)], axis=1)
    return d.transpose(0, 1, 3, 2)


def _merge_kernel(x_ref, mod_ref, nw_ref, pc_ref, pprev_ref, pnext_ref, gc_ref, cw_ref,
                  yna_ref, ygla_ref, wg_ref, bg_ref, wb_ref, wo_ref, o_ref, *, tiles_per_seq):
    i = pl.program_id(0)
    x = x_ref[...]
    h = _modulated_norm(x, mod_ref, nw_ref).astype(BF16)
    tm = x.shape[0]

    pc = pc_ref[...].astype(F32)
    pos = i % tiles_per_seq
    prev_row = jnp.where(pos == 0, 0.0, pprev_ref[HALO - 1:HALO, :].astype(F32))
    next_row = jnp.where(pos == tiles_per_seq - 1, 0.0, pnext_ref[0:1, :].astype(F32))
    ridx = lax.broadcasted_iota(jnp.int32, (tm, 1), 0)
    before = jnp.where(ridx == 0, prev_row, pltpu.roll(pc, 1, axis=0))
    after = jnp.where(ridx == tm - 1, next_row, pltpu.roll(pc, tm - 1, axis=0))
    conv = before * cw_ref[0:1, :] + pc * cw_ref[1:2, :] + after * cw_ref[2:3, :]
    y_conv = (gc_ref[...].astype(F32) * conv).astype(BF16)

    merged = None
    for b, yb in enumerate((y_conv, yna_ref[...], ygla_ref[...])):
        cols = slice(b * D_MODEL, (b + 1) * D_MODEL)
        g = _sigmoid(_dot(h, wg_ref[:, cols]) + bg_ref[:, cols])
        t = g * _dot(yb, wb_ref[b])
        merged = t if merged is None else merged + t
    gate = mod_ref[0, :, 2 * D_MODEL:3 * D_MODEL]
    o_ref[...] = x + gate * _dot(merged.astype(BF16), wo_ref[...])


def _merge(x, mod, nw, pc, gc, cw, yna, ygla, wg, bg, wb, wo, seq):
    n_tok = x.shape[0]
    tm = min(TOKEN_TILE, seq)
    tiles_per_seq = seq // tm
    hb = tm // HALO
    last = n_tok // HALO - 1

    def tok(w):
        return pl.BlockSpec((tm, w), lambda i: (i, 0))

    def const(shape):
        return pl.BlockSpec(shape, lambda i: (0,) * len(shape))

    return pl.pallas_call(
        functools.partial(_merge_kernel, tiles_per_seq=tiles_per_seq),
        grid=(n_tok // tm,),
        in_specs=[
            tok(D_MODEL),
            pl.BlockSpec((1, 1, 3 * D_MODEL), lambda i: (i // tiles_per_seq, 0, 0)),
            const((1, D_MODEL)),
            tok(BRANCH_W),
            pl.BlockSpec((HALO, BRANCH_W), lambda i: (jnp.maximum(i * hb - 1, 0), 0)),
            pl.BlockSpec((HALO, BRANCH_W), lambda i: (jnp.minimum((i + 1) * hb, last), 0)),
            tok(BRANCH_W),
            const((3, BRANCH_W)),
            tok(BRANCH_W),
            tok(BRANCH_W),
            const((D_MODEL, N_BRANCH * D_MODEL)),
            const((1, N_BRANCH * D_MODEL)),
            const((N_BRANCH, BRANCH_W, D_MODEL)),
            const((D_MODEL, D_MODEL)),
        ],
        out_specs=tok(D_MODEL),
        out_shape=jax.ShapeDtypeStruct((n_tok, D_MODEL), F32),
        compiler_params=_cparams(1),
        name="merge",
    )(x, mod, nw, pc, pc, pc, gc, cw, yna, ygla, wg, bg, wb, wo)


def _rope_tables(seq):
    pos = np.arange(seq)
    n_f = GLA_DK // 4
    inv = ROPE_BASE ** (-np.arange(n_f) / n_f)
    ang_r = ((pos // GRID_W)[:, None] * inv).astype(np.float32).astype(np.float64)
    ang_c = ((pos % GRID_W)[:, None] * inv).astype(np.float32).astype(np.float64)
    zero = np.zeros_like(ang_r)
    cos = np.concatenate([np.cos(ang_r), np.cos(ang_r), np.cos(ang_c), np.cos(ang_c)], axis=1)
    s1 = np.concatenate([-np.sin(ang_r), zero, -np.sin(ang_c), zero], axis=1)
    s2 = np.concatenate([zero, np.sin(ang_r), zero, np.sin(ang_c)], axis=1)
    tile = lambda t: jnp.asarray(np.tile(t, (1, LANES // GLA_DK)).astype(np.float32))
    return tile(cos), tile(s1), tile(s2)


def _head_mean_matrix():
    h = np.arange(BRANCH_W) // NA_HD
    return jnp.asarray((h[:, None] == h[None, :]).astype(np.float32) / NA_HD, BF16)


def _gla_blockdiag_mask():
    r = np.arange(GLA_HEADS * GLA_DV) // GLA_DV
    c = np.arange(GLA_KW) // GLA_DK
    return jnp.asarray((r[:, None] == c[None, :]).astype(np.float32), F32)


def kernel(x_prompt, x_sample, c, cache_k, cache_v, state_fwd, state_bwd, c_ctx,
           norm_w, w_ada, b_ada, w_in, conv_w, q_norm_w, k_norm_w, rpb,
           w_alpha, b_alpha, gla_norm_w, w_branch, w_gate, b_gate, w_out):
    nb_p, seq_p, _ = x_prompt.shape
    nb_s, seq_s, _ = x_sample.shape
    past = cache_k.shape[3]

    cond = jnp.concatenate([c, c_ctx[None, :], jnp.zeros((16 - nb_s - 1, D_MODEL), F32)], axis=0)
    mods = _adaln(cond, w_ada, b_ada)

    w_in_b = jnp.pad(w_in, ((0, 0), (0, 0), (0, D_IN_PAD - D_IN))).astype(BF16)
    w_gate_b = w_gate.astype(BF16)
    w_branch_b = w_branch.astype(BF16)
    w_out_b = w_out.astype(BF16)
    hm = _head_mean_matrix()
    bd = _gla_blockdiag_mask()
    rope_tabs = _rope_tables(seq_s)

    ck = cache_k.transpose(0, 1, 3, 2, 4).reshape(nb_s, DEPTH, past, BRANCH_W).astype(BF16)
    cv = cache_v.transpose(0, 1, 3, 2, 4).reshape(nb_s, DEPTH, past, BRANCH_W).astype(BF16)

    y_p = x_prompt.reshape(nb_p * seq_p, D_MODEL)
    y_s = x_sample.reshape(nb_s * seq_s, D_MODEL)
    zeros_state = jnp.zeros((nb_p, GLA_HEADS * GLA_DV, GLA_KW), F32)
    ks, vs, sfs, sbs = [], [], [], []
    for l in range(DEPTH):
        nw = norm_w[l].reshape(1, D_MODEL)
        qw = jnp.tile(q_norm_w[l], NA_HEADS).reshape(1, BRANCH_W)
        kw = jnp.tile(k_norm_w[l], NA_HEADS).reshape(1, BRANCH_W)
        gw = jnp.tile(gla_norm_w[l], GLA_HEADS).reshape(1, BRANCH_W)
        wa = jnp.zeros((LANES, 2 * GLA_KW), F32)
        wa = wa.at[0:GLA_RANK, 0:GLA_KW].set(w_alpha[l, 0])
        wa = wa.at[GLA_RANK:2 * GLA_RANK, GLA_KW:].set(w_alpha[l, 1])
        ba = b_alpha[l].reshape(1, 2 * GLA_KW)
        bg = b_gate[l].reshape(1, N_BRANCH * D_MODEL)

        mod_p = mods[l, nb_s:nb_s + 1].reshape(1, 1, 3 * D_MODEL)
        mod_p = jnp.broadcast_to(mod_p, (nb_p, 1, 3 * D_MODEL))
        (pc, gc, q, k, v, gn, qg, kg, vg, gg, lr) = _inproj(
            y_p, mod_p, nw, w_in_b[l], hm, qw, kw, None, seq_p, F32)
        yna = _na_ctx(q, k, v, gn, nb_p, seq_p)
        ygla, sf, sb = _gla(qg, kg, vg, gg, lr, wa, ba, gw, bd, zeros_state, zeros_state,
                            nb_p, seq_p)
        y_p = _merge(y_p, mod_p, nw, pc, gc, conv_w[l], yna, ygla,
                     w_gate_b[l], bg, w_branch_b[l], w_out_b[l], seq_p)
        ks.append(k.reshape(nb_p, seq_p, NA_HEADS, NA_HD).transpose(0, 2, 1, 3))
        vs.append(v.reshape(nb_p, seq_p, NA_HEADS, NA_HD).transpose(0, 2, 1, 3))
        sfs.append(_blockdiag_t_to_state(sf))
        sbs.append(_blockdiag_t_to_state(sb))

        mod_s = mods[l, 0:nb_s].reshape(nb_s, 1, 3 * D_MODEL)
        (pc, gc, q, k, v, gn, qg, kg, vg, gg, lr) = _inproj(
            y_s, mod_s, nw, w_in_b[l], hm, qw, kw, rope_tabs, seq_s, BF16)
        bias = _na_bias_table(rpb[l], seq_s // GRID_W)
        yna = _na_lat(q, k, v, ck[:, l], cv[:, l], bias, gn, nb_s, seq_s)
        ygla, _, _ = _gla(qg, kg, vg, gg, lr, wa, ba, gw, bd,
                          _state_to_blockdiag_t(state_fwd[:, l]),
                          _state_to_blockdiag_t(state_bwd[:, l]), nb_s, seq_s)
        y_s = _merge(y_s, mod_s, nw, pc, gc, conv_w[l], yna, ygla,
                     w_gate_b[l], bg, w_branch_b[l], w_out_b[l], seq_s)

    return (y_p.reshape(nb_p, seq_p, D_MODEL), y_s.reshape(nb_s, seq_s, D_MODEL),
            jnp.stack(ks, axis=1), jnp.stack(vs, axis=1),
            jnp.stack(sfs, axis=1), jnp.stack(sbs, axis=1))
```

```python
import functools

import numpy as np
import jax
import jax.numpy as jnp
from jax import lax
from jax.experimental import pallas as pl
from jax.experimental.pallas import tpu as pltpu

F32 = jnp.float32
BF16 = jnp.bfloat16

D_MODEL = 1024
DEPTH = 4
GRID_W = 64
EPS = 1e-6
NEG_INF = -1e30
BRANCH_W = D_MODEL // 2
N_BRANCH = 3
NA_HEADS = 8
NA_HD = BRANCH_W // NA_HEADS
WIN_R = 8
WIN_C = 16
GLA_HEADS = 4
GLA_DV = BRANCH_W // GLA_HEADS
GLA_DK = GLA_DV // 2
GLA_KW = GLA_HEADS * GLA_DK
GLA_RANK = 16
GLA_TAU = 16.0
ROPE_BASE = 10000.0

LANES = 128
D_IN = 8 * BRANCH_W + 2 * GLA_KW + 2 * BRANCH_W + 2 * GLA_RANK
D_IN_PAD = ((D_IN + LANES - 1) // LANES) * LANES
C_XA, C_BA, C_CA, C_GA = 0, 512, 1024, 1536
C_QN, C_KN, C_VN, C_GN = 2048, 2560, 3072, 3584
C_QG, C_KG, C_VG, C_GG, C_LR = 4096, 4352, 4608, 5120, 5632

VMEM_LIMIT = 56 * 1024 * 1024
TOKEN_TILE = 512
GLA_CHUNK = 64
NA_QROWS = 4
NA_KROWS = 12
HALO = 16


def _cparams(n_axes):
    return pltpu.CompilerParams(dimension_semantics=("arbitrary",) * n_axes,
                                vmem_limit_bytes=VMEM_LIMIT)


def _sigmoid(x):
    return 1.0 / (1.0 + jnp.exp(-x))


def _silu(x):
    return x * _sigmoid(x)


def _log_sigmoid(x):
    return jnp.minimum(x, 0.0) - jnp.log(1.0 + jnp.exp(-jnp.abs(x)))


def _dot(a, b):
    return jnp.dot(a, b, preferred_element_type=F32)


def _dot_nt(a, b):
    return lax.dot_general(a, b, (((1,), (1,)), ((), ())), preferred_element_type=F32)


def _dot_tn(a, b):
    return lax.dot_general(a, b, (((0,), (0,)), ((), ())), preferred_element_type=F32)


def _ada_kernel(cond_ref, w_ref, b_ref, o_ref):
    c = cond_ref[...]
    o_ref[0] = _dot(_silu(c).astype(BF16), w_ref[0].astype(BF16)) + b_ref[0]


def _adaln(cond, w_ada, b_ada):
    r = cond.shape[0]
    tn = 768
    return pl.pallas_call(
        _ada_kernel,
        grid=(DEPTH, 3 * D_MODEL // tn),
        in_specs=[
            pl.BlockSpec((r, D_MODEL), lambda l, n: (0, 0)),
            pl.BlockSpec((1, D_MODEL, tn), lambda l, n: (l, 0, n)),
            pl.BlockSpec((1, 1, tn), lambda l, n: (l, 0, n)),
        ],
        out_specs=pl.BlockSpec((1, r, tn), lambda l, n: (l, 0, n)),
        out_shape=jax.ShapeDtypeStruct((DEPTH, r, 3 * D_MODEL), F32),
        compiler_params=_cparams(2),
        name="adaln",
    )(cond, w_ada, b_ada.reshape(DEPTH, 1, 3 * D_MODEL))


def _modulated_norm(x, mod_ref, nw_ref):
    ms = jnp.mean(x * x, axis=-1, keepdims=True)
    xn = x * lax.rsqrt(ms + EPS) * nw_ref[...]
    shift = mod_ref[0, :, 0:D_MODEL]
    scale = mod_ref[0, :, D_MODEL:2 * D_MODEL]
    return xn * (1.0 + scale) + shift


def _inproj_kernel(*refs, rope):
    if rope:
        (x_ref, mod_ref, nw_ref, w_ref, hm_ref, qw_ref, kw_ref, rc_ref, rs1_ref, rs2_ref,
         pc_ref, gc_ref, q_ref, k_ref, v_ref, gn_ref, qg_ref, kg_ref, vg_ref, gg_ref, lr_ref) = refs
    else:
        (x_ref, mod_ref, nw_ref, w_ref, hm_ref, qw_ref, kw_ref,
         pc_ref, gc_ref, q_ref, k_ref, v_ref, gn_ref, qg_ref, kg_ref, vg_ref, gg_ref, lr_ref) = refs
    h = _modulated_norm(x_ref[...], mod_ref, nw_ref).astype(BF16)

    def proj(c0, n):
        return _dot(h, w_ref[:, c0:c0 + n])

    pc_ref[...] = (proj(C_CA, BRANCH_W) * proj(C_XA, BRANCH_W)).astype(pc_ref.dtype)
    gc_ref[...] = (proj(C_BA, BRANCH_W) * _silu(proj(C_GA, BRANCH_W))).astype(gc_ref.dtype)

    def head_norm(u, g_ref):
        ms = _dot((u * u).astype(BF16), hm_ref[...])
        return u * lax.rsqrt(ms + EPS) * g_ref[...]

    q_ref[...] = (head_norm(proj(C_QN, BRANCH_W), qw_ref) * (NA_HD ** -0.5)).astype(q_ref.dtype)
    k_ref[...] = head_norm(proj(C_KN, BRANCH_W), kw_ref).astype(k_ref.dtype)
    v_ref[...] = proj(C_VN, BRANCH_W).astype(v_ref.dtype)
    gn_ref[...] = _silu(proj(C_GN, BRANCH_W)).astype(gn_ref.dtype)

    qg = proj(C_QG, GLA_KW) * (GLA_DK ** -0.5)
    kg = proj(C_KG, GLA_KW)
    if rope:
        def rot(u):
            parts = []
            for j in range(GLA_KW // LANES):
                uj = u[:, j * LANES:(j + 1) * LANES]
                parts.append(uj * rc_ref[...]
                             + pltpu.roll(uj, LANES - GLA_DK // 4, axis=1) * rs1_ref[...]
                             + pltpu.roll(uj, GLA_DK // 4, axis=1) * rs2_ref[...])
            return jnp.concatenate(parts, axis=1)
        qg = rot(qg)
        kg = rot(kg)
    qg_ref[...] = qg.astype(qg_ref.dtype)
    kg_ref[...] = kg.astype(kg_ref.dtype)
    vg_ref[...] = proj(C_VG, BRANCH_W).astype(vg_ref.dtype)
    gg_ref[...] = _silu(proj(C_GG, BRANCH_W)).astype(gg_ref.dtype)
    lr_ref[...] = proj(C_LR, LANES)


def _inproj(x, mod, nw, w_in, hm, qw, kw, rope_tabs, seq, kv_dtype):
    n_tok = x.shape[0]
    tm = min(TOKEN_TILE, seq)
    tiles_per_seq = seq // tm
    rope = rope_tabs is not None

    def tok(w):
        return pl.BlockSpec((tm, w), lambda i: (i, 0))

    def const(shape):
        return pl.BlockSpec(shape, lambda i: (0,) * len(shape))

    in_specs = [
        tok(D_MODEL),
        pl.BlockSpec((1, 1, 3 * D_MODEL), lambda i: (i // tiles_per_seq, 0, 0)),
        const((1, D_MODEL)),
        const((D_MODEL, D_IN_PAD)),
        const((BRANCH_W, BRANCH_W)),
        const((1, BRANCH_W)),
        const((1, BRANCH_W)),
    ]
    args = [x, mod, nw, w_in, hm, qw, kw]
    if rope:
        in_specs += [pl.BlockSpec((tm, LANES), lambda i: (i % tiles_per_seq, 0))] * 3
        args += list(rope_tabs)
    widths = [BRANCH_W, BRANCH_W, BRANCH_W, BRANCH_W, BRANCH_W, BRANCH_W,
              GLA_KW, GLA_KW, BRANCH_W, BRANCH_W, LANES]
    dtypes = [BF16, BF16, BF16, kv_dtype, kv_dtype, BF16, BF16, BF16, BF16, BF16, F32]
    return pl.pallas_call(
        functools.partial(_inproj_kernel, rope=rope),
        grid=(n_tok // tm,),
        in_specs=in_specs,
        out_specs=[tok(w) for w in widths],
        out_shape=[jax.ShapeDtypeStruct((n_tok, w), d) for w, d in zip(widths, dtypes)],
        compiler_params=_cparams(1),
        name="inproj_rope" if rope else "inproj",
    )(*args)


def _pair_masks(dtype):
    lane = lax.broadcasted_iota(jnp.int32, (1, LANES), 1)
    return lane < NA_HD, lane >= NA_HD


def _softmax_pv(scores, values):
    m = functools.reduce(jnp.maximum, [jnp.max(s, axis=-1, keepdims=True) for s in scores])
    ps = [jnp.exp(s - m) for s in scores]
    l = functools.reduce(lambda a, b: a + b, [jnp.sum(p, axis=-1, keepdims=True) for p in ps])
    o = functools.reduce(lambda a, b: a + b,
                         [_dot(p.astype(BF16), v) for p, v in zip(ps, values)])
    return o / l


def _na_ctx_kernel(q_ref, k_ref, v_ref, gn_ref, o_ref):
    m0, m1 = _pair_masks(BF16)
    for hp in range(NA_HEADS // 2):
        cols = slice(hp * LANES, (hp + 1) * LANES)
        qp = q_ref[0, :, cols]
        kp = k_ref[0, :, cols].astype(BF16)
        vp = v_ref[0, :, cols].astype(BF16)
        outs = []
        for msk in (m0, m1):
            qa = jnp.where(msk, qp, jnp.zeros_like(qp))
            outs.append(_softmax_pv([_dot_nt(qa, kp)], [vp]))
        o = jnp.where(m0, outs[0], outs[1])
        o_ref[0, :, cols] = (o * gn_ref[0, :, cols].astype(F32)).astype(o_ref.dtype)


def _na_ctx(q, k, v, gn, batch, seq):
    def blk():
        return pl.BlockSpec((1, seq, BRANCH_W), lambda b: (b, 0, 0))
    r3 = lambda u: u.reshape(batch, seq, BRANCH_W)
    out = pl.pallas_call(
        _na_ctx_kernel,
        grid=(batch,),
        in_specs=[blk(), blk(), blk(), blk()],
        out_specs=blk(),
        out_shape=jax.ShapeDtypeStruct((batch, seq, BRANCH_W), BF16),
        compiler_params=_cparams(1),
        name="na_ctx",
    )(r3(q), r3(k), r3(v), r3(gn))
    return out.reshape(batch * seq, BRANCH_W)


def _na_window_start(j, rows):
    return jnp.clip(j * NA_QROWS - WIN_R // 2, 0, rows - NA_KROWS)


def _na_lat_kernel(q_ref, k_ref, v_ref, kc_ref, vc_ref, bias_ref, gn_ref, o_ref, *, rows):
    j = pl.program_id(2)
    start = pl.multiple_of(_na_window_start(j, rows) * GRID_W, NA_QROWS * GRID_W)
    m0, m1 = _pair_masks(BF16)
    qp = q_ref[0]
    kw = k_ref[0, pl.ds(start, NA_KROWS * GRID_W), :]
    vw = v_ref[0, pl.ds(start, NA_KROWS * GRID_W), :]
    kc = kc_ref[0]
    vc = vc_ref[0]
    outs = []
    for a, msk in enumerate((m0, m1)):
        qa = jnp.where(msk, qp, jnp.zeros_like(qp))
        s_w = _dot_nt(qa, kw) + bias_ref[0, a].astype(F32)
        s_c = _dot_nt(qa, kc)
        outs.append(_softmax_pv([s_w, s_c], [vw, vc]))
    o = jnp.where(m0, outs[0], outs[1])
    o_ref[0] = (o * gn_ref[0].astype(F32)).astype(o_ref.dtype)


def _na_lat(q, k, v, kc, vc, bias, gn, batch, seq):
    rows = seq // GRID_W
    nj = rows // NA_QROWS
    tq = NA_QROWS * GRID_W

    def bias_type(j):
        return jnp.where(j == 0, 0, jnp.where(j == nj - 1, 2, 1))

    qblk = pl.BlockSpec((1, tq, LANES), lambda b, hp, j: (b, j, hp))
    seqblk = pl.BlockSpec((1, seq, LANES), lambda b, hp, j: (b, 0, hp))
    ctxblk = pl.BlockSpec((1, kc.shape[1], LANES), lambda b, hp, j: (b, 0, hp))
    r3 = lambda u: u.reshape(batch, seq, BRANCH_W)
    out = pl.pallas_call(
        functools.partial(_na_lat_kernel, rows=rows),
        grid=(batch, NA_HEADS // 2, nj),
        in_specs=[qblk, seqblk, seqblk, ctxblk, ctxblk,
                  pl.BlockSpec((1, 2, tq, NA_KROWS * GRID_W),
                               lambda b, hp, j: (bias_type(j), hp, 0, 0)),
                  qblk],
        out_specs=qblk,
        out_shape=jax.ShapeDtypeStruct((batch, seq, BRANCH_W), BF16),
        compiler_params=_cparams(3),
        name="na_lat",
    )(r3(q), r3(k), r3(v), kc, vc, bias, r3(gn))
    return out.reshape(batch * seq, BRANCH_W)


def _na_bias_kernel(rpb_ref, o_ref, *, rows):
    nj = rows // NA_QROWS
    qc = lax.broadcasted_iota(jnp.int32, (GRID_W, LANES), 0)
    lane = lax.broadcasted_iota(jnp.int32, (GRID_W, LANES), 1)
    kc = lane & (GRID_W - 1)
    cs = jnp.clip(qc - WIN_C // 2, 0, GRID_W - WIN_C)
    col_ok = (kc >= cs) & (kc < cs + WIN_C)
    left = lane < GRID_W
    neg = jnp.full((GRID_W, LANES), NEG_INF, F32)
    tl, tr = [], []
    for d in range(2 * WIN_R - 1):
        x = jnp.broadcast_to(rpb_ref[0, d:d + 1, :], (GRID_W, LANES))
        tl.append(pltpu.roll(x, LANES - (WIN_C - 1), 1, stride=1, stride_axis=0))
        tr.append(pltpu.roll(x, GRID_W - (WIN_C - 1), 1, stride=1, stride_axis=0))
    for t, j in enumerate((0, 1, nj - 1)):
        start = int(np.clip(j * NA_QROWS - WIN_R // 2, 0, rows - NA_KROWS))
        for rl in range(NA_QROWS):
            r = j * NA_QROWS + rl
            rs = int(np.clip(r - WIN_R // 2, 0, rows - WIN_R))
            for kp in range(NA_KROWS // 2):
                k0 = start + 2 * kp
                a = tl[k0 - r + WIN_R - 1] if rs <= k0 < rs + WIN_R else neg
                b = tr[k0 + 1 - r + WIN_R - 1] if rs <= k0 + 1 < rs + WIN_R else neg
                blk = jnp.where(col_ok, jnp.where(left, a, b), NEG_INF)
                o_ref[t, 0, rl * GRID_W:(rl + 1) * GRID_W, kp * LANES:(kp + 1) * LANES] = (
                    blk.astype(o_ref.dtype))


def _na_bias_table(rpb, rows):
    rp = jnp.pad(rpb, ((0, 0), (0, 0), (0, LANES - (2 * WIN_C - 1))))
    tq, tk = NA_QROWS * GRID_W, NA_KROWS * GRID_W
    return pl.pallas_call(
        functools.partial(_na_bias_kernel, rows=rows),
        grid=(NA_HEADS,),
        in_specs=[pl.BlockSpec((1, 2 * WIN_R - 1, LANES), lambda h: (h, 0, 0))],
        out_specs=pl.BlockSpec((3, 1, tq, tk), lambda h: (0, h, 0, 0)),
        out_shape=jax.ShapeDtypeStruct((3, NA_HEADS, tq, tk), BF16),
        compiler_params=_cparams(1),
        name="na_bias",
    )(rp)


def _gla_kernel(q_ref, k_ref, v_ref, gg_ref, lr_ref, wa_ref, ba_ref, gw_ref, bd_ref,
                s0f_ref, s0b_ref, y_ref, sf_ref, sb_ref, of_ref, ob_ref, stf_ref, stb_ref, *, seq):
    c = GLA_CHUNK
    n = seq // c
    row = lax.broadcasted_iota(jnp.int32, (c, c), 0)
    col = lax.broadcasted_iota(jnp.int32, (c, c), 1)
    lower = row >= col
    tri_f = jnp.where(lower, 1.0, 0.0).astype(BF16)
    tri_b = jnp.where(row <= col, 1.0, 0.0).astype(BF16)
    lane = lax.broadcasted_iota(jnp.int32, (1, GLA_KW), 1)
    head_masks = [(lane >= h * GLA_DK) & (lane < (h + 1) * GLA_DK) for h in range(GLA_HEADS)]
    wa = wa_ref[...].astype(BF16)

    stf_ref[...] = s0f_ref[0]
    stb_ref[...] = s0b_ref[0]

    def chunk(r0, la_cols, tri, causal, ref_row, end_row, st_ref, out_ref):
        sl = pl.ds(r0, c)
        z = _dot(lr_ref[0, sl, :].astype(BF16), wa[:, la_cols]) + ba_ref[:, la_cols]
        la = _log_sigmoid(z) * (1.0 / GLA_TAU)
        la_hi = la.astype(BF16)
        la_lo = (la - la_hi.astype(F32)).astype(BF16)
        bc = _dot(tri, la_hi) + _dot(tri, la_lo)
        ref = bc[ref_row:ref_row + 1, :]
        end = bc[end_row:end_row + 1, :]
        qd = q_ref[0, sl, :].astype(F32) * jnp.exp(bc - ref)
        kd = k_ref[0, sl, :].astype(F32) * jnp.exp(ref - bc)
        kd_b = kd.astype(BF16)
        vc = v_ref[0, sl, :]
        st = st_ref[...]
        o = _dot_nt((qd * jnp.exp(ref)).astype(BF16), st.astype(BF16))
        qd_b = qd.astype(BF16)
        parts = []
        for h in range(GLA_HEADS):
            qh = jnp.where(head_masks[h], qd_b, jnp.zeros_like(qd_b))
            att = jnp.where(causal, _dot_nt(qh, kd_b), 0.0)
            parts.append(_dot(att.astype(BF16), vc[:, h * GLA_DV:(h + 1) * GLA_DV]))
        out_ref[sl, :] = o + jnp.concatenate(parts, axis=1)
        upd = _dot_tn(vc, (kd * jnp.exp(end - ref)).astype(BF16))
        st_ref[...] = st * jnp.exp(end) + upd * bd_ref[...]

    def body(i, carry):
        rf = pl.multiple_of(i * c, c)
        rb = pl.multiple_of((n - 1 - i) * c, c)
        chunk(rf, slice(0, GLA_KW), tri_f, lower, c // 2 - 1, c - 1, stf_ref, of_ref)
        chunk(rb, slice(GLA_KW, 2 * GLA_KW), tri_b, row <= col, c // 2, 0, stb_ref, ob_ref)
        return carry

    lax.fori_loop(0, n, body, 0)
    sf_ref[0] = stf_ref[...]
    sb_ref[0] = stb_ref[...]

    def finish(i, carry):
        sl = pl.ds(pl.multiple_of(i * c, c), c)
        o = of_ref[sl, :] + ob_ref[sl, :]
        parts = []
        for h in range(GLA_HEADS):
            oh = o[:, h * GLA_DV:(h + 1) * GLA_DV]
            ms = jnp.mean(oh * oh, axis=-1, keepdims=True)
            parts.append(oh * lax.rsqrt(ms + EPS))
        y = jnp.concatenate(parts, axis=1) * gw_ref[...]
        y_ref[0, sl, :] = (y * gg_ref[0, sl, :].astype(F32)).astype(y_ref.dtype)
        return carry

    lax.fori_loop(0, n, finish, 0)


def _gla(qg, kg, vg, gg, lr, wa, ba, gw, bd, s0f, s0b, batch, seq):
    hv, hk = GLA_HEADS * GLA_DV, GLA_KW

    def seqblk(w):
        return pl.BlockSpec((1, seq, w), lambda b: (b, 0, 0))

    def const(shape):
        return pl.BlockSpec(shape, lambda b: (0,) * len(shape))

    stblk = pl.BlockSpec((1, hv, hk), lambda b: (b, 0, 0))
    r3 = lambda u: u.reshape(batch, seq, u.shape[-1])
    y, sf, sb = pl.pallas_call(
        functools.partial(_gla_kernel, seq=seq),
        grid=(batch,),
        in_specs=[seqblk(hk), seqblk(hk), seqblk(hv), seqblk(hv), seqblk(LANES),
                  const((LANES, 2 * hk)), const((1, 2 * hk)), const((1, hv)), const((hv, hk)),
                  stblk, stblk],
        out_specs=[seqblk(hv), stblk, stblk],
        out_shape=[jax.ShapeDtypeStruct((batch, seq, hv), BF16),
                   jax.ShapeDtypeStruct((batch, hv, hk), F32),
                   jax.ShapeDtypeStruct((batch, hv, hk), F32)],
        scratch_shapes=[pltpu.VMEM((seq, hv), F32), pltpu.VMEM((seq, hv), F32),
                        pltpu.VMEM((hv, hk), F32), pltpu.VMEM((hv, hk), F32)],
        compiler_params=_cparams(1),
        name="gla",
    )(r3(qg), r3(kg), r3(vg), r3(gg), r3(lr), wa, ba, gw, bd, s0f, s0b)
    return y.reshape(batch * seq, hv), sf, sb


def _state_to_blockdiag_t(s):
    b = s.shape[0]
    eye = jnp.eye(GLA_HEADS, dtype=s.dtype)
    t = jnp.einsum("bhkv,hg->bhvgk", s, eye)
    return t.reshape(b, GLA_HEADS * GLA_DV, GLA_KW)


def _blockdiag_t_to_state(t):
    b = t.shape[0]
    t5 = t.reshape(b, GLA_HEADS, GLA_DV, GLA_HEADS, GLA_DK)
    d = jnp.stack([t5[:, h, :, h, :] for h in range(GLA_HEADS)], axis=1)
    return d.transpose(0, 1, 3, 2)


def _merge_kernel(x_ref, mod_ref, nw_ref, pc_ref, pprev_ref, pnext_ref, gc_ref, cw_ref,
                  yna_ref, ygla_ref, wg_ref, bg_ref, wb_ref, wo_ref, o_ref, *, tiles_per_seq):
    i = pl.program_id(0)
    x = x_ref[...]
    h = _modulated_norm(x, mod_ref, nw_ref).astype(BF16)
    tm = x.shape[0]

    pc = pc_ref[...].astype(F32)
    pos = i % tiles_per_seq
    prev_row = jnp.where(pos == 0, 0.0, pprev_ref[HALO - 1:HALO, :].astype(F32))
    next_row = jnp.where(pos == tiles_per_seq - 1, 0.0, pnext_ref[0:1, :].astype(F32))
    ridx = lax.broadcasted_iota(jnp.int32, (tm, 1), 0)
    before = jnp.where(ridx == 0, prev_row, pltpu.roll(pc, 1, axis=0))
    after = jnp.where(ridx == tm - 1, next_row, pltpu.roll(pc, tm - 1, axis=0))
    conv = before * cw_ref[0:1, :] + pc * cw_ref[1:2, :] + after * cw_ref[2:3, :]
    y_conv = (gc_ref[...].astype(F32) * conv).astype(BF16)

    merged = None
    for b, yb in enumerate((y_conv, yna_ref[...], ygla_ref[...])):
        cols = slice(b * D_MODEL, (b + 1) * D_MODEL)
        g = _sigmoid(_dot(h, wg_ref[:, cols]) + bg_ref[:, cols])
        t = g * _dot(yb, wb_ref[b])
        merged = t if merged is None else merged + t
    gate = mod_ref[0, :, 2 * D_MODEL:3 * D_MODEL]
    o_ref[...] = x + gate * _dot(merged.astype(BF16), wo_ref[...])


def _merge(x, mod, nw, pc, gc, cw, yna, ygla, wg, bg, wb, wo, seq):
    n_tok = x.shape[0]
    tm = min(TOKEN_TILE, seq)
    tiles_per_seq = seq // tm
    hb = tm // HALO
    last = n_tok // HALO - 1

    def tok(w):
        return pl.BlockSpec((tm, w), lambda i: (i, 0))

    def const(shape):
        return pl.BlockSpec(shape, lambda i: (0,) * len(shape))

    return pl.pallas_call(
        functools.partial(_merge_kernel, tiles_per_seq=tiles_per_seq),
        grid=(n_tok // tm,),
        in_specs=[
            tok(D_MODEL),
            pl.BlockSpec((1, 1, 3 * D_MODEL), lambda i: (i // tiles_per_seq, 0, 0)),
            const((1, D_MODEL)),
            tok(BRANCH_W),
            pl.BlockSpec((HALO, BRANCH_W), lambda i: (jnp.maximum(i * hb - 1, 0), 0)),
            pl.BlockSpec((HALO, BRANCH_W), lambda i: (jnp.minimum((i + 1) * hb, last), 0)),
            tok(BRANCH_W),
            const((3, BRANCH_W)),
            tok(BRANCH_W),
            tok(BRANCH_W),
            const((D_MODEL, N_BRANCH * D_MODEL)),
            const((1, N_BRANCH * D_MODEL)),
            const((N_BRANCH, BRANCH_W, D_MODEL)),
            const((D_MODEL, D_MODEL)),
        ],
        out_specs=tok(D_MODEL),
        out_shape=jax.ShapeDtypeStruct((n_tok, D_MODEL), F32),
        compiler_params=_cparams(1),
        name="merge",
    )(x, mod, nw, pc, pc, pc, gc, cw, yna, ygla, wg, bg, wb, wo)


def _rope_tables(seq):
    pos = np.arange(seq)
    n_f = GLA_DK // 4
    inv = ROPE_BASE ** (-np.arange(n_f) / n_f)
    ang_r = ((pos // GRID_W)[:, None] * inv).astype(np.float32).astype(np.float64)
    ang_c = ((pos % GRID_W)[:, None] * inv).astype(np.float32).astype(np.float64)
    zero = np.zeros_like(ang_r)
    cos = np.concatenate([np.cos(ang_r), np.cos(ang_r), np.cos(ang_c), np.cos(ang_c)], axis=1)
    s1 = np.concatenate([-np.sin(ang_r), zero, -np.sin(ang_c), zero], axis=1)
    s2 = np.concatenate([zero, np.sin(ang_r), zero, np.sin(ang_c)], axis=1)
    tile = lambda t: jnp.asarray(np.tile(t, (1, LANES // GLA_DK)).astype(np.float32))
    return tile(cos), tile(s1), tile(s2)


def _head_mean_matrix():
    h = np.arange(BRANCH_W) // NA_HD
    return jnp.asarray((h[:, None] == h[None, :]).astype(np.float32) / NA_HD, BF16)


def _gla_blockdiag_mask():
    r = np.arange(GLA_HEADS * GLA_DV) // GLA_DV
    c = np.arange(GLA_KW) // GLA_DK
    return jnp.asarray((r[:, None] == c[None, :]).astype(np.float32), F32)


def kernel(x_prompt, x_sample, c, cache_k, cache_v, state_fwd, state_bwd, c_ctx,
           norm_w, w_ada, b_ada, w_in, conv_w, q_norm_w, k_norm_w, rpb,
           w_alpha, b_alpha, gla_norm_w, w_branch, w_gate, b_gate, w_out):
    nb_p, seq_p, _ = x_prompt.shape
    nb_s, seq_s, _ = x_sample.shape
    past = cache_k.shape[3]

    cond = jnp.concatenate([c, c_ctx[None, :], jnp.zeros((16 - nb_s - 1, D_MODEL), F32)], axis=0)
    mods = _adaln(cond, w_ada, b_ada)

    w_in_b = jnp.pad(w_in, ((0, 0), (0, 0), (0, D_IN_PAD - D_IN))).astype(BF16)
    w_gate_b = w_gate.astype(BF16)
    w_branch_b = w_branch.astype(BF16)
    w_out_b = w_out.astype(BF16)
    hm = _head_mean_matrix()
    bd = _gla_blockdiag_mask()
    rope_tabs = _rope_tables(seq_s)

    ck = cache_k.transpose(0, 1, 3, 2, 4).reshape(nb_s, DEPTH, past, BRANCH_W).astype(BF16)
    cv = cache_v.transpose(0, 1, 3, 2, 4).reshape(nb_s, DEPTH, past, BRANCH_W).astype(BF16)

    y_p = x_prompt.reshape(nb_p * seq_p, D_MODEL)
    y_s = x_sample.reshape(nb_s * seq_s, D_MODEL)
    zeros_state = jnp.zeros((nb_p, GLA_HEADS * GLA_DV, GLA_KW), F32)
    ks, vs, sfs, sbs = [], [], [], []
    for l in range(DEPTH):
        nw = norm_w[l].reshape(1, D_MODEL)
        qw = jnp.tile(q_norm_w[l], NA_HEADS).reshape(1, BRANCH_W)
        kw = jnp.tile(k_norm_w[l], NA_HEADS).reshape(1, BRANCH_W)
        gw = jnp.tile(gla_norm_w[l], GLA_HEADS).reshape(1, BRANCH_W)
        wa = jnp.zeros((LANES, 2 * GLA_KW), F32)
        wa = wa.at[0:GLA_RANK, 0:GLA_KW].set(w_alpha[l, 0])
        wa = wa.at[GLA_RANK:2 * GLA_RANK, GLA_KW:].set(w_alpha[l, 1])
        ba = b_alpha[l].reshape(1, 2 * GLA_KW)
        bg = b_gate[l].reshape(1, N_BRANCH * D_MODEL)

        mod_p = mods[l, nb_s:nb_s + 1].reshape(1, 1, 3 * D_MODEL)
        mod_p = jnp.broadcast_to(mod_p, (nb_p, 1, 3 * D_MODEL))
        (pc, gc, q, k, v, gn, qg, kg, vg, gg, lr) = _inproj(
            y_p, mod_p, nw, w_in_b[l], hm, qw, kw, None, seq_p, F32)
        yna = _na_ctx(q, k, v, gn, nb_p, seq_p)
        ygla, sf, sb = _gla(qg, kg, vg, gg, lr, wa, ba, gw, bd, zeros_state, zeros_state,
                            nb_p, seq_p)
        y_p = _merge(y_p, mod_p, nw, pc, gc, conv_w[l], yna, ygla,
                     w_gate_b[l], bg, w_branch_b[l], w_out_b[l], seq_p)
        ks.append(k.reshape(nb_p, seq_p, NA_HEADS, NA_HD).transpose(0, 2, 1, 3))
        vs.append(v.reshape(nb_p, seq_p, NA_HEADS, NA_HD).transpose(0, 2, 1, 3))
        sfs.append(_blockdiag_t_to_state(sf))
        sbs.append(_blockdiag_t_to_state(sb))

        mod_s = mods[l, 0:nb_s].reshape(nb_s, 1, 3 * D_MODEL)
        (pc, gc, q, k, v, gn, qg, kg, vg, gg, lr) = _inproj(
            y_s, mod_s, nw, w_in_b[l], hm, qw, kw, rope_tabs, seq_s, BF16)
        bias = _na_bias_table(rpb[l], seq_s // GRID_W)
        yna = _na_lat(q, k, v, ck[:, l], cv[:, l], bias, gn, nb_s, seq_s)
        ygla, _, _ = _gla(qg, kg, vg, gg, lr, wa, ba, gw, bd,
                          _state_to_blockdiag_t(state_fwd[:, l]),
                          _state_to_blockdiag_t(state_bwd[:, l]), nb_s, seq_s)
        y_s = _merge(y_s, mod_s, nw, pc, gc, conv_w[l], yna, ygla,
                     w_gate_b[l], bg, w_branch_b[l], w_out_b[l], seq_s)

    return (y_p.reshape(nb_p, seq_p, D_MODEL), y_s.reshape(nb_s, seq_s, D_MODEL),
            jnp.stack(ks, axis=1), jnp.stack(vs, axis=1),
            jnp.stack(sfs, axis=1), jnp.stack(sbs, axis=1))
```

```python
import functools

import numpy as np
import jax
import jax.numpy as jnp
from jax import lax
from jax.experimental import pallas as pl
from jax.experimental.pallas import tpu as pltpu

F32 = jnp.float32
BF16 = jnp.bfloat16

D_MODEL = 1024
DEPTH = 4
GRID_W = 64
EPS = 1e-6
NEG_INF = -1e30
BRANCH_W = D_MODEL // 2
N_BRANCH = 3
NA_HEADS = 8
NA_HD = BRANCH_W // NA_HEADS
WIN_R = 8
WIN_C = 16
GLA_HEADS = 4
GLA_DV = BRANCH_W // GLA_HEADS
GLA_DK = GLA_DV // 2
GLA_KW = GLA_HEADS * GLA_DK
GLA_RANK = 16
GLA_TAU = 16.0
ROPE_BASE = 10000.0

LANES = 128
D_IN = 8 * BRANCH_W + 2 * GLA_KW + 2 * BRANCH_W + 2 * GLA_RANK
D_IN_PAD = ((D_IN + LANES - 1) // LANES) * LANES
C_XA, C_BA, C_CA, C_GA = 0, 512, 1024, 1536
C_QN, C_KN, C_VN, C_GN = 2048, 2560, 3072, 3584
C_QG, C_KG, C_VG, C_GG, C_LR = 4096, 4352, 4608, 5120, 5632

VMEM_LIMIT = 56 * 1024 * 1024
TOKEN_TILE = 512
GLA_CHUNK = 64
GLA_TILE = 256
NA_QROWS = 4
NA_KROWS = 12
HALO = 16


def _cparams(n_axes):
    return pltpu.CompilerParams(dimension_semantics=("arbitrary",) * n_axes,
                                vmem_limit_bytes=VMEM_LIMIT)


def _sigmoid(x):
    return 1.0 / (1.0 + jnp.exp(-x))


def _silu(x):
    return x * _sigmoid(x)


def _log_sigmoid(x):
    return jnp.minimum(x, 0.0) - jnp.log(1.0 + jnp.exp(-jnp.abs(x)))


def _dot(a, b):
    return jnp.dot(a, b, preferred_element_type=F32)


def _dot_nt(a, b):
    return lax.dot_general(a, b, (((1,), (1,)), ((), ())), preferred_element_type=F32)


def _dot_tn(a, b):
    return lax.dot_general(a, b, (((0,), (0,)), ((), ())), preferred_element_type=F32)


def _ada_kernel(cond_ref, w_ref, b_ref, o_ref):
    c = cond_ref[...]
    o_ref[0] = _dot(_silu(c).astype(BF16), w_ref[0].astype(BF16)) + b_ref[0]


def _adaln(cond, w_ada, b_ada):
    r = cond.shape[0]
    tn = 768
    return pl.pallas_call(
        _ada_kernel,
        grid=(DEPTH, 3 * D_MODEL // tn),
        in_specs=[
            pl.BlockSpec((r, D_MODEL), lambda l, n: (0, 0)),
            pl.BlockSpec((1, D_MODEL, tn), lambda l, n: (l, 0, n)),
            pl.BlockSpec((1, 1, tn), lambda l, n: (l, 0, n)),
        ],
        out_specs=pl.BlockSpec((1, r, tn), lambda l, n: (l, 0, n)),
        out_shape=jax.ShapeDtypeStruct((DEPTH, r, 3 * D_MODEL), F32),
        compiler_params=_cparams(2),
        name="adaln",
    )(cond, w_ada, b_ada.reshape(DEPTH, 1, 3 * D_MODEL))


def _modulated_norm(x, mod_ref, nw_ref):
    ms = jnp.mean(x * x, axis=-1, keepdims=True)
    xn = x * lax.rsqrt(ms + EPS) * nw_ref[...]
    shift = mod_ref[0, :, 0:D_MODEL]
    scale = mod_ref[0, :, D_MODEL:2 * D_MODEL]
    return xn * (1.0 + scale) + shift


def _inproj_kernel(*refs, rope):
    if rope:
        (x_ref, mod_ref, nw_ref, w_ref, hm_ref, qw_ref, kw_ref, rc_ref, rs1_ref, rs2_ref,
         pc_ref, gc_ref, q_ref, k_ref, v_ref, gn_ref, qg_ref, kg_ref, vg_ref, gg_ref, lr_ref) = refs
    else:
        (x_ref, mod_ref, nw_ref, w_ref, hm_ref, qw_ref, kw_ref,
         pc_ref, gc_ref, q_ref, k_ref, v_ref, gn_ref, qg_ref, kg_ref, vg_ref, gg_ref, lr_ref) = refs
    h = _modulated_norm(x_ref[...], mod_ref, nw_ref).astype(BF16)

    def proj(c0, n):
        return _dot(h, w_ref[:, c0:c0 + n])

    pc_ref[...] = (proj(C_CA, BRANCH_W) * proj(C_XA, BRANCH_W)).astype(pc_ref.dtype)
    gc_ref[...] = (proj(C_BA, BRANCH_W) * _silu(proj(C_GA, BRANCH_W))).astype(gc_ref.dtype)

    def head_norm(u, g_ref):
        ms = _dot((u * u).astype(BF16), hm_ref[...])
        return u * lax.rsqrt(ms + EPS) * g_ref[...]

    q_ref[...] = (head_norm(proj(C_QN, BRANCH_W), qw_ref) * (NA_HD ** -0.5)).astype(q_ref.dtype)
    k_ref[...] = head_norm(proj(C_KN, BRANCH_W), kw_ref).astype(k_ref.dtype)
    v_ref[...] = proj(C_VN, BRANCH_W).astype(v_ref.dtype)
    gn_ref[...] = _silu(proj(C_GN, BRANCH_W)).astype(gn_ref.dtype)

    qg = proj(C_QG, GLA_KW) * (GLA_DK ** -0.5)
    kg = proj(C_KG, GLA_KW)
    if rope:
        def rot(u):
            parts = []
            for j in range(GLA_KW // LANES):
                uj = u[:, j * LANES:(j + 1) * LANES]
                parts.append(uj * rc_ref[...]
                             + pltpu.roll(uj, LANES - GLA_DK // 4, axis=1) * rs1_ref[...]
                             + pltpu.roll(uj, GLA_DK // 4, axis=1) * rs2_ref[...])
            return jnp.concatenate(parts, axis=1)
        qg = rot(qg)
        kg = rot(kg)
    qg_ref[...] = qg.astype(qg_ref.dtype)
    kg_ref[...] = kg.astype(kg_ref.dtype)
    vg_ref[...] = proj(C_VG, BRANCH_W).astype(vg_ref.dtype)
    gg_ref[...] = _silu(proj(C_GG, BRANCH_W)).astype(gg_ref.dtype)
    lr_ref[...] = proj(C_LR, LANES)


def _inproj(x, mod, nw, w_in, hm, qw, kw, rope_tabs, seq, kv_dtype):
    n_tok = x.shape[0]
    tm = min(TOKEN_TILE, seq)
    tiles_per_seq = seq // tm
    rope = rope_tabs is not None

    def tok(w):
        return pl.BlockSpec((tm, w), lambda i: (i, 0))

    def const(shape):
        return pl.BlockSpec(shape, lambda i: (0,) * len(shape))

    in_specs = [
        tok(D_MODEL),
        pl.BlockSpec((1, 1, 3 * D_MODEL), lambda i: (i // tiles_per_seq, 0, 0)),
        const((1, D_MODEL)),
        const((D_MODEL, D_IN_PAD)),
        const((BRANCH_W, BRANCH_W)),
        const((1, BRANCH_W)),
        const((1, BRANCH_W)),
    ]
    args = [x, mod, nw, w_in, hm, qw, kw]
    if rope:
        in_specs += [pl.BlockSpec((tm, LANES), lambda i: (i % tiles_per_seq, 0))] * 3
        args += list(rope_tabs)
    widths = [BRANCH_W, BRANCH_W, BRANCH_W, BRANCH_W, BRANCH_W, BRANCH_W,
              GLA_KW, GLA_KW, BRANCH_W, BRANCH_W, LANES]
    dtypes = [BF16, BF16, BF16, kv_dtype, kv_dtype, BF16, BF16, BF16, BF16, BF16, F32]
    return pl.pallas_call(
        functools.partial(_inproj_kernel, rope=rope),
        grid=(n_tok // tm,),
        in_specs=in_specs,
        out_specs=[tok(w) for w in widths],
        out_shape=[jax.ShapeDtypeStruct((n_tok, w), d) for w, d in zip(widths, dtypes)],
        compiler_params=_cparams(1),
        name="inproj_rope" if rope else "inproj",
    )(*args)


def _pair_masks(dtype):
    lane = lax.broadcasted_iota(jnp.int32, (1, LANES), 1)
    return lane < NA_HD, lane >= NA_HD


def _softmax_pv(scores, values):
    m = functools.reduce(jnp.maximum, [jnp.max(s, axis=-1, keepdims=True) for s in scores])
    ps = [jnp.exp(s - m) for s in scores]
    l = functools.reduce(lambda a, b: a + b, [jnp.sum(p, axis=-1, keepdims=True) for p in ps])
    o = functools.reduce(lambda a, b: a + b,
                         [_dot(p.astype(BF16), v) for p, v in zip(ps, values)])
    return o / l


def _na_ctx_kernel(q_ref, k_ref, v_ref, gn_ref, o_ref):
    m0, m1 = _pair_masks(BF16)
    for hp in range(NA_HEADS // 2):
        cols = slice(hp * LANES, (hp + 1) * LANES)
        qp = q_ref[0, :, cols]
        kp = k_ref[0, :, cols].astype(BF16)
        vp = v_ref[0, :, cols].astype(BF16)
        outs = []
        for msk in (m0, m1):
            qa = jnp.where(msk, qp, jnp.zeros_like(qp))
            outs.append(_softmax_pv([_dot_nt(qa, kp)], [vp]))
        o = jnp.where(m0, outs[0], outs[1])
        o_ref[0, :, cols] = (o * gn_ref[0, :, cols].astype(F32)).astype(o_ref.dtype)


def _na_ctx(q, k, v, gn, batch, seq):
    def blk():
        return pl.BlockSpec((1, seq, BRANCH_W), lambda b: (b, 0, 0))
    r3 = lambda u: u.reshape(batch, seq, BRANCH_W)
    out = pl.pallas_call(
        _na_ctx_kernel,
        grid=(batch,),
        in_specs=[blk(), blk(), blk(), blk()],
        out_specs=blk(),
        out_shape=jax.ShapeDtypeStruct((batch, seq, BRANCH_W), BF16),
        compiler_params=_cparams(1),
        name="na_ctx",
    )(r3(q), r3(k), r3(v), r3(gn))
    return out.reshape(batch * seq, BRANCH_W)


def _na_window_start(j, rows):
    return jnp.clip(j * NA_QROWS - WIN_R // 2, 0, rows - NA_KROWS)


def _na_lat_kernel(q_ref, k_ref, v_ref, kc_ref, vc_ref, bias_ref, gn_ref, o_ref, *, rows):
    j = pl.program_id(2)
    start = pl.multiple_of(_na_window_start(j, rows) * GRID_W, NA_QROWS * GRID_W)
    m0, m1 = _pair_masks(BF16)
    qp = q_ref[0]
    kw = k_ref[0, pl.ds(start, NA_KROWS * GRID_W), :]
    vw = v_ref[0, pl.ds(start, NA_KROWS * GRID_W), :]
    kc = kc_ref[0]
    vc = vc_ref[0]
    outs = []
    for a, msk in enumerate((m0, m1)):
        qa = jnp.where(msk, qp, jnp.zeros_like(qp))
        s_w = _dot_nt(qa, kw) + bias_ref[0, a].astype(F32)
        s_c = _dot_nt(qa, kc)
        outs.append(_softmax_pv([s_w, s_c], [vw, vc]))
    o = jnp.where(m0, outs[0], outs[1])
    o_ref[0] = (o * gn_ref[0].astype(F32)).astype(o_ref.dtype)


def _na_lat(q, k, v, kc, vc, bias, gn, batch, seq):
    rows = seq // GRID_W
    nj = rows // NA_QROWS
    tq = NA_QROWS * GRID_W

    def bias_type(j):
        return jnp.where(j == 0, 0, jnp.where(j == nj - 1, 2, 1))

    qblk = pl.BlockSpec((1, tq, LANES), lambda b, hp, j: (b, j, hp))
    seqblk = pl.BlockSpec((1, seq, LANES), lambda b, hp, j: (b, 0, hp))
    ctxblk = pl.BlockSpec((1, kc.shape[1], LANES), lambda b, hp, j: (b, 0, hp))
    r3 = lambda u: u.reshape(batch, seq, BRANCH_W)
    out = pl.pallas_call(
        functools.partial(_na_lat_kernel, rows=rows),
        grid=(batch, NA_HEADS // 2, nj),
        in_specs=[qblk, seqblk, seqblk, ctxblk, ctxblk,
                  pl.BlockSpec((1, 2, tq, NA_KROWS * GRID_W),
                               lambda b, hp, j: (bias_type(j), hp, 0, 0)),
                  qblk],
        out_specs=qblk,
        out_shape=jax.ShapeDtypeStruct((batch, seq, BRANCH_W), BF16),
        compiler_params=_cparams(3),
        name="na_lat",
    )(r3(q), r3(k), r3(v), kc, vc, bias, r3(gn))
    return out.reshape(batch * seq, BRANCH_W)


def _na_bias_kernel(rpb_ref, o_ref, *, rows):
    nj = rows // NA_QROWS
    qc = lax.broadcasted_iota(jnp.int32, (GRID_W, LANES), 0)
    lane = lax.broadcasted_iota(jnp.int32, (GRID_W, LANES), 1)
    kc = lane & (GRID_W - 1)
    cs = jnp.clip(qc - WIN_C // 2, 0, GRID_W - WIN_C)
    col_ok = (kc >= cs) & (kc < cs + WIN_C)
    left = lane < GRID_W
    neg = jnp.full((GRID_W, LANES), NEG_INF, F32)
    tl, tr = [], []
    for d in range(2 * WIN_R - 1):
        x = jnp.broadcast_to(rpb_ref[0, d:d + 1, :], (GRID_W, LANES))
        tl.append(pltpu.roll(x, LANES - (WIN_C - 1), 1, stride=1, stride_axis=0))
        tr.append(pltpu.roll(x, GRID_W - (WIN_C - 1), 1, stride=1, stride_axis=0))
    for t, j in enumerate((0, 1, nj - 1)):
        start = int(np.clip(j * NA_QROWS - WIN_R // 2, 0, rows - NA_KROWS))
        for rl in range(NA_QROWS):
            r = j * NA_QROWS + rl
            rs = int(np.clip(r - WIN_R // 2, 0, rows - WIN_R))
            for kp in range(NA_KROWS // 2):
                k0 = start + 2 * kp
                a = tl[k0 - r + WIN_R - 1] if rs <= k0 < rs + WIN_R else neg
                b = tr[k0 + 1 - r + WIN_R - 1] if rs <= k0 + 1 < rs + WIN_R else neg
                blk = jnp.where(col_ok, jnp.where(left, a, b), NEG_INF)
                o_ref[t, 0, rl * GRID_W:(rl + 1) * GRID_W, kp * LANES:(kp + 1) * LANES] = (
                    blk.astype(o_ref.dtype))


def _na_bias_table(rpb, rows):
    rp = jnp.pad(rpb, ((0, 0), (0, 0), (0, LANES - (2 * WIN_C - 1))))
    tq, tk = NA_QROWS * GRID_W, NA_KROWS * GRID_W
    return pl.pallas_call(
        functools.partial(_na_bias_kernel, rows=rows),
        grid=(NA_HEADS,),
        in_specs=[pl.BlockSpec((1, 2 * WIN_R - 1, LANES), lambda h: (h, 0, 0))],
        out_specs=pl.BlockSpec((3, 1, tq, tk), lambda h: (0, h, 0, 0)),
        out_shape=jax.ShapeDtypeStruct((3, NA_HEADS, tq, tk), BF16),
        compiler_params=_cparams(1),
        name="na_bias",
    )(rp)


def _gla_kernel(q_ref, k_ref, v_ref, gg_ref, lr_ref, wa_ref, ba_ref, gw_ref, bd_ref, pm_ref,
                s0f_ref, s0b_ref, y_ref, sf_ref, sb_ref,
                of_ref, ob_ref, stf_ref, stb_ref, qi_ref, ku_ref, dec_ref, *, seq):
    c = GLA_CHUNK
    n = seq // c
    t = GLA_TILE
    lane = lax.broadcasted_iota(jnp.int32, (1, GLA_KW), 1)
    head_masks = [(lane >= h * GLA_DK) & (lane < (h + 1) * GLA_DK) for h in range(GLA_HEADS)]
    row = lax.broadcasted_iota(jnp.int32, (t, t), 0)
    col = lax.broadcasted_iota(jnp.int32, (t, t), 1)
    same_chunk = (row // c) == (col // c)
    causal = (same_chunk & (row >= col), same_chunk & (row <= col))
    wa = wa_ref[...].astype(BF16)
    outs = (of_ref, ob_ref)

    def prep(i, carry):
        sl = pl.ds(pl.multiple_of(i * t, t), t)
        lr = lr_ref[0, sl, :].astype(BF16)
        q = q_ref[0, sl, :].astype(F32)
        k = k_ref[0, sl, :].astype(F32)
        v = v_ref[0, sl, :]
        for d in range(2):
            cols = slice(d * GLA_KW, (d + 1) * GLA_KW)
            la = _log_sigmoid(_dot(lr, wa[:, cols]) + ba_ref[:, cols]) * (1.0 / GLA_TAU)
            la_hi = la.astype(BF16)
            la_lo = (la - la_hi.astype(F32)).astype(BF16)
            g = _dot(pm_ref[d], la_hi) + _dot(pm_ref[d], la_lo)
            g_ref, g_cum, g_rest, g_all = (g[j * t:(j + 1) * t] for j in range(4))
            qd = (q * jnp.exp(g_ref)).astype(BF16)
            kd = (k * jnp.exp(-g_ref)).astype(BF16)
            qi_ref[d, sl, :] = (q * jnp.exp(g_cum)).astype(BF16)
            ku_ref[d, sl, :] = (k * jnp.exp(g_rest)).astype(BF16)
            dec_ref[d, sl, :] = jnp.exp(g_all)
            parts = []
            for h in range(GLA_HEADS):
                qh = jnp.where(head_masks[h], qd, jnp.zeros_like(qd))
                att = jnp.where(causal[d], _dot_nt(qh, kd), 0.0)
                parts.append(_dot(att.astype(BF16), v[:, h * GLA_DV:(h + 1) * GLA_DV]))
            outs[d][sl, :] = jnp.concatenate(parts, axis=1)
        return carry

    lax.fori_loop(0, seq // t, prep, 0)

    stf_ref[...] = s0f_ref[0]
    stb_ref[...] = s0b_ref[0]

    def step(d, r0, st_ref):
        sl = pl.ds(r0, c)
        st = st_ref[...]
        outs[d][sl, :] += _dot_nt(qi_ref[d, sl, :], st.astype(BF16))
        upd = _dot_tn(v_ref[0, sl, :], ku_ref[d, sl, :])
        st_ref[...] = st * dec_ref[d, pl.ds(r0, 1), :] + upd * bd_ref[...]

    def scan(i, carry):
        step(0, pl.multiple_of(i * c, c), stf_ref)
        step(1, pl.multiple_of((n - 1 - i) * c, c), stb_ref)
        return carry

    lax.fori_loop(0, n, scan, 0)
    sf_ref[0] = stf_ref[...]
    sb_ref[0] = stb_ref[...]

    def finish(i, carry):
        sl = pl.ds(pl.multiple_of(i * t, t), t)
        o = of_ref[sl, :] + ob_ref[sl, :]
        parts = []
        for h in range(GLA_HEADS):
            oh = o[:, h * GLA_DV:(h + 1) * GLA_DV]
            ms = jnp.mean(oh * oh, axis=-1, keepdims=True)
            parts.append(oh * lax.rsqrt(ms + EPS))
        y = jnp.concatenate(parts, axis=1) * gw_ref[...]
        y_ref[0, sl, :] = (y * gg_ref[0, sl, :].astype(F32)).astype(y_ref.dtype)
        return carry

    lax.fori_loop(0, seq // t, finish, 0)


def _gla_sum_matrices():
    t, c = GLA_TILE, GLA_CHUNK
    r = np.arange(t)[:, None]
    j = np.arange(t)[None, :]
    same = (r // c) == (j // c)
    rl, jl = r % c, j % c
    mats = []
    for cum, ref in (((jl <= rl), (jl <= c // 2 - 1)), ((jl >= rl), (jl >= c // 2))):
        cum = (same & cum).astype(np.float32)
        ref = (same & ref).astype(np.float32)
        full = same.astype(np.float32)
        mats.append(np.concatenate([cum - ref, cum, full - cum, full], axis=0))
    return jnp.asarray(np.stack(mats).astype(np.float32), BF16)


def _gla(qg, kg, vg, gg, lr, wa, ba, gw, bd, pm, s0f, s0b, batch, seq):
    hv, hk = GLA_HEADS * GLA_DV, GLA_KW
    assert seq % GLA_TILE == 0

    def seqblk(w):
        return pl.BlockSpec((1, seq, w), lambda b: (b, 0, 0))

    def const(shape):
        return pl.BlockSpec(shape, lambda b: (0,) * len(shape))

    stblk = pl.BlockSpec((1, hv, hk), lambda b: (b, 0, 0))
    r3 = lambda u: u.reshape(batch, seq, u.shape[-1])
    y, sf, sb = pl.pallas_call(
        functools.partial(_gla_kernel, seq=seq),
        grid=(batch,),
        in_specs=[seqblk(hk), seqblk(hk), seqblk(hv), seqblk(hv), seqblk(LANES),
                  const((LANES, 2 * hk)), const((1, 2 * hk)), const((1, hv)), const((hv, hk)),
                  const((2, 4 * GLA_TILE, GLA_TILE)), stblk, stblk],
        out_specs=[seqblk(hv), stblk, stblk],
        out_shape=[jax.ShapeDtypeStruct((batch, seq, hv), BF16),
                   jax.ShapeDtypeStruct((batch, hv, hk), F32),
                   jax.ShapeDtypeStruct((batch, hv, hk), F32)],
        scratch_shapes=[pltpu.VMEM((seq, hv), F32), pltpu.VMEM((seq, hv), F32),
                        pltpu.VMEM((hv, hk), F32), pltpu.VMEM((hv, hk), F32),
                        pltpu.VMEM((2, seq, hk), BF16), pltpu.VMEM((2, seq, hk), BF16),
                        pltpu.VMEM((2, seq, hk), F32)],
        compiler_params=_cparams(1),
        name="gla",
    )(r3(qg), r3(kg), r3(vg), r3(gg), r3(lr), wa, ba, gw, bd, pm, s0f, s0b)
    return y.reshape(batch * seq, hv), sf, sb


def _state_to_blockdiag_t(s):
    b = s.shape[0]
    eye = jnp.eye(GLA_HEADS, dtype=s.dtype)
    t = jnp.einsum("bhkv,hg->bhvgk", s, eye)
    return t.reshape(b, GLA_HEADS * GLA_DV, GLA_KW)


def _blockdiag_t_to_state(t):
    b = t.shape[0]
    t5 = t.reshape(b, GLA_HEADS, GLA_DV, GLA_HEADS, GLA_DK)
    d = jnp.stack([t5[:, h, :, h, :] for h in range(GLA_HEADS)], axis=1)
    return d.transpose(0, 1, 3, 2)


def _merge_kernel(x_ref, mod_ref, nw_ref, pc_ref, pprev_ref, pnext_ref, gc_ref, cw_ref,
                  yna_ref, ygla_ref, wg_ref, bg_ref, wb_ref, wo_ref, o_ref, *, tiles_per_seq):
    i = pl.program_id(0)
    x = x_ref[...]
    h = _modulated_norm(x, mod_ref, nw_ref).astype(BF16)
    tm = x.shape[0]

    pc = pc_ref[...].astype(F32)
    pos = i % tiles_per_seq
    prev_row = jnp.where(pos == 0, 0.0, pprev_ref[HALO - 1:HALO, :].astype(F32))
    next_row = jnp.where(pos == tiles_per_seq - 1, 0.0, pnext_ref[0:1, :].astype(F32))
    ridx = lax.broadcasted_iota(jnp.int32, (tm, 1), 0)
    before = jnp.where(ridx == 0, prev_row, pltpu.roll(pc, 1, axis=0))
    after = jnp.where(ridx == tm - 1, next_row, pltpu.roll(pc, tm - 1, axis=0))
    conv = before * cw_ref[0:1, :] + pc * cw_ref[1:2, :] + after * cw_ref[2:3, :]
    y_conv = (gc_ref[...].astype(F32) * conv).astype(BF16)

    merged = None
    for b, yb in enumerate((y_conv, yna_ref[...], ygla_ref[...])):
        cols = slice(b * D_MODEL, (b + 1) * D_MODEL)
        g = _sigmoid(_dot(h, wg_ref[:, cols]) + bg_ref[:, cols])
        t = g * _dot(yb, wb_ref[b])
        merged = t if merged is None else merged + t
    gate = mod_ref[0, :, 2 * D_MODEL:3 * D_MODEL]
    o_ref[...] = x + gate * _dot(merged.astype(BF16), wo_ref[...])


def _merge(x, mod, nw, pc, gc, cw, yna, ygla, wg, bg, wb, wo, seq):
    n_tok = x.shape[0]
    tm = min(TOKEN_TILE, seq)
    tiles_per_seq = seq // tm
    hb = tm // HALO
    last = n_tok // HALO - 1

    def tok(w):
        return pl.BlockSpec((tm, w), lambda i: (i, 0))

    def const(shape):
        return pl.BlockSpec(shape, lambda i: (0,) * len(shape))

    return pl.pallas_call(
        functools.partial(_merge_kernel, tiles_per_seq=tiles_per_seq),
        grid=(n_tok // tm,),
        in_specs=[
            tok(D_MODEL),
            pl.BlockSpec((1, 1, 3 * D_MODEL), lambda i: (i // tiles_per_seq, 0, 0)),
            const((1, D_MODEL)),
            tok(BRANCH_W),
            pl.BlockSpec((HALO, BRANCH_W), lambda i: (jnp.maximum(i * hb - 1, 0), 0)),
            pl.BlockSpec((HALO, BRANCH_W), lambda i: (jnp.minimum((i + 1) * hb, last), 0)),
            tok(BRANCH_W),
            const((3, BRANCH_W)),
            tok(BRANCH_W),
            tok(BRANCH_W),
            const((D_MODEL, N_BRANCH * D_MODEL)),
            const((1, N_BRANCH * D_MODEL)),
            const((N_BRANCH, BRANCH_W, D_MODEL)),
            const((D_MODEL, D_MODEL)),
        ],
        out_specs=tok(D_MODEL),
        out_shape=jax.ShapeDtypeStruct((n_tok, D_MODEL), F32),
        compiler_params=_cparams(1),
        name="merge",
    )(x, mod, nw, pc, pc, pc, gc, cw, yna, ygla, wg, bg, wb, wo)


def _rope_tables(seq):
    pos = np.arange(seq)
    n_f = GLA_DK // 4
    inv = ROPE_BASE ** (-np.arange(n_f) / n_f)
    ang_r = ((pos // GRID_W)[:, None] * inv).astype(np.float32).astype(np.float64)
    ang_c = ((pos % GRID_W)[:, None] * inv).astype(np.float32).astype(np.float64)
    zero = np.zeros_like(ang_r)
    cos = np.concatenate([np.cos(ang_r), np.cos(ang_r), np.cos(ang_c), np.cos(ang_c)], axis=1)
    s1 = np.concatenate([-np.sin(ang_r), zero, -np.sin(ang_c), zero], axis=1)
    s2 = np.concatenate([zero, np.sin(ang_r), zero, np.sin(ang_c)], axis=1)
    tile = lambda t: jnp.asarray(np.tile(t, (1, LANES // GLA_DK)).astype(np.float32))
    return tile(cos), tile(s1), tile(s2)


def _head_mean_matrix():
    h = np.arange(BRANCH_W) // NA_HD
    return jnp.asarray((h[:, None] == h[None, :]).astype(np.float32) / NA_HD, BF16)


def _gla_blockdiag_mask():
    r = np.arange(GLA_HEADS * GLA_DV) // GLA_DV
    c = np.arange(GLA_KW) // GLA_DK
    return jnp.asarray((r[:, None] == c[None, :]).astype(np.float32), F32)


def kernel(x_prompt, x_sample, c, cache_k, cache_v, state_fwd, state_bwd, c_ctx,
           norm_w, w_ada, b_ada, w_in, conv_w, q_norm_w, k_norm_w, rpb,
           w_alpha, b_alpha, gla_norm_w, w_branch, w_gate, b_gate, w_out):
    nb_p, seq_p, _ = x_prompt.shape
    nb_s, seq_s, _ = x_sample.shape
    past = cache_k.shape[3]

    cond = jnp.concatenate([c, c_ctx[None, :], jnp.zeros((16 - nb_s - 1, D_MODEL), F32)], axis=0)
    mods = _adaln(cond, w_ada, b_ada)

    w_in_b = jnp.pad(w_in, ((0, 0), (0, 0), (0, D_IN_PAD - D_IN))).astype(BF16)
    w_gate_b = w_gate.astype(BF16)
    w_branch_b = w_branch.astype(BF16)
    w_out_b = w_out.astype(BF16)
    hm = _head_mean_matrix()
    bd = _gla_blockdiag_mask()
    pm = _gla_sum_matrices()
    rope_tabs = _rope_tables(seq_s)

    ck = cache_k.transpose(0, 1, 3, 2, 4).reshape(nb_s, DEPTH, past, BRANCH_W).astype(BF16)
    cv = cache_v.transpose(0, 1, 3, 2, 4).reshape(nb_s, DEPTH, past, BRANCH_W).astype(BF16)

    y_p = x_prompt.reshape(nb_p * seq_p, D_MODEL)
    y_s = x_sample.reshape(nb_s * seq_s, D_MODEL)
    zeros_state = jnp.zeros((nb_p, GLA_HEADS * GLA_DV, GLA_KW), F32)
    ks, vs, sfs, sbs = [], [], [], []
    for l in range(DEPTH):
        nw = norm_w[l].reshape(1, D_MODEL)
        qw = jnp.tile(q_norm_w[l], NA_HEADS).reshape(1, BRANCH_W)
        kw = jnp.tile(k_norm_w[l], NA_HEADS).reshape(1, BRANCH_W)
        gw = jnp.tile(gla_norm_w[l], GLA_HEADS).reshape(1, BRANCH_W)
        wa = jnp.zeros((LANES, 2 * GLA_KW), F32)
        wa = wa.at[0:GLA_RANK, 0:GLA_KW].set(w_alpha[l, 0])
        wa = wa.at[GLA_RANK:2 * GLA_RANK, GLA_KW:].set(w_alpha[l, 1])
        ba = b_alpha[l].reshape(1, 2 * GLA_KW)
        bg = b_gate[l].reshape(1, N_BRANCH * D_MODEL)

        mod_p = mods[l, nb_s:nb_s + 1].reshape(1, 1, 3 * D_MODEL)
        mod_p = jnp.broadcast_to(mod_p, (nb_p, 1, 3 * D_MODEL))
        (pc, gc, q, k, v, gn, qg, kg, vg, gg, lr) = _inproj(
            y_p, mod_p, nw, w_in_b[l], hm, qw, kw, None, seq_p, F32)
        yna = _na_ctx(q, k, v, gn, nb_p, seq_p)
        ygla, sf, sb = _gla(qg, kg, vg, gg, lr, wa, ba, gw, bd, pm, zeros_state, zeros_state,
                            nb_p, seq_p)
        y_p = _merge(y_p, mod_p, nw, pc, gc, conv_w[l], yna, ygla,
                     w_gate_b[l], bg, w_branch_b[l], w_out_b[l], seq_p)
        ks.append(k.reshape(nb_p, seq_p, NA_HEADS, NA_HD).transpose(0, 2, 1, 3))
        vs.append(v.reshape(nb_p, seq_p, NA_HEADS, NA_HD).transpose(0, 2, 1, 3))
        sfs.append(_blockdiag_t_to_state(sf))
        sbs.append(_blockdiag_t_to_state(sb))

        mod_s = mods[l, 0:nb_s].reshape(nb_s, 1, 3 * D_MODEL)
        (pc, gc, q, k, v, gn, qg, kg, vg, gg, lr) = _inproj(
            y_s, mod_s, nw, w_in_b[l], hm, qw, kw, rope_tabs, seq_s, BF16)
        bias = _na_bias_table(rpb[l], seq_s // GRID_W)
        yna = _na_lat(q, k, v, ck[:, l], cv[:, l], bias, gn, nb_s, seq_s)
        ygla, _, _ = _gla(qg, kg, vg, gg, lr, wa, ba, gw, bd, pm,
                          _state_to_blockdiag_t(state_fwd[:, l]),
                          _state_to_blockdiag_t(state_bwd[:, l]), nb_s, seq_s)
        y_s = _merge(y_s, mod_s, nw, pc, gc, conv_w[l], yna, ygla,
                     w_gate_b[l], bg, w_branch_b[l], w_out_b[l], seq_s)

    return (y_p.reshape(nb_p, seq_p, D_MODEL), y_s.reshape(nb_s, seq_s, D_MODEL),
            jnp.stack(ks, axis=1), jnp.stack(vs, axis=1),
            jnp.stack(sfs, axis=1), jnp.stack(sbs, axis=1))
```

```python
import functools

import numpy as np
import jax
import jax.numpy as jnp
from jax import lax
from jax.experimental import pallas as pl
from jax.experimental.pallas import tpu as pltpu

F32 = jnp.float32
BF16 = jnp.bfloat16

D_MODEL = 1024
DEPTH = 4
GRID_W = 64
EPS = 1e-6
NEG_INF = -1e30
BRANCH_W = D_MODEL // 2
N_BRANCH = 3
NA_HEADS = 8
NA_HD = BRANCH_W // NA_HEADS
WIN_R = 8
WIN_C = 16
GLA_HEADS = 4
GLA_DV = BRANCH_W // GLA_HEADS
GLA_DK = GLA_DV // 2
GLA_KW = GLA_HEADS * GLA_DK
GLA_RANK = 16
GLA_TAU = 16.0
ROPE_BASE = 10000.0
LOG2E = 1.4426950408889634

LANES = 128
D_IN = 8 * BRANCH_W + 2 * GLA_KW + 2 * BRANCH_W + 2 * GLA_RANK
D_IN_PAD = ((D_IN + LANES - 1) // LANES) * LANES
C_XA, C_BA, C_CA, C_GA = 0, 512, 1024, 1536
C_QN, C_KN, C_VN, C_GN = 2048, 2560, 3072, 3584
C_QG, C_KG, C_VG, C_GG, C_LR = 4096, 4352, 4608, 5120, 5632

VMEM_LIMIT = 56 * 1024 * 1024
TOKEN_TILE = 512
GLA_CHUNK = 64
GLA_TILE = 256
NA_QROWS = 4
NA_KROWS = 12
HALO = 16


def _cparams(n_axes):
    return pltpu.CompilerParams(dimension_semantics=("arbitrary",) * n_axes,
                                vmem_limit_bytes=VMEM_LIMIT)


def _sigmoid(x):
    return 1.0 / (1.0 + jnp.exp(-x))


def _silu(x):
    return x * _sigmoid(x)


def _log_sigmoid(x):
    return jnp.minimum(x, 0.0) - jnp.log(1.0 + jnp.exp(-jnp.abs(x)))


def _dot(a, b):
    return jnp.dot(a, b, preferred_element_type=F32)


def _dot_nt(a, b):
    return lax.dot_general(a, b, (((1,), (1,)), ((), ())), preferred_element_type=F32)


def _dot_tn(a, b):
    return lax.dot_general(a, b, (((0,), (0,)), ((), ())), preferred_element_type=F32)


def _ada_kernel(cond_ref, w_ref, b_ref, o_ref):
    c = cond_ref[...]
    o_ref[0] = _dot(_silu(c).astype(BF16), w_ref[0].astype(BF16)) + b_ref[0]


def _adaln(cond, w_ada, b_ada):
    r = cond.shape[0]
    tn = 768
    return pl.pallas_call(
        _ada_kernel,
        grid=(DEPTH, 3 * D_MODEL // tn),
        in_specs=[
            pl.BlockSpec((r, D_MODEL), lambda l, n: (0, 0)),
            pl.BlockSpec((1, D_MODEL, tn), lambda l, n: (l, 0, n)),
            pl.BlockSpec((1, 1, tn), lambda l, n: (l, 0, n)),
        ],
        out_specs=pl.BlockSpec((1, r, tn), lambda l, n: (l, 0, n)),
        out_shape=jax.ShapeDtypeStruct((DEPTH, r, 3 * D_MODEL), F32),
        compiler_params=_cparams(2),
        name="adaln",
    )(cond, w_ada, b_ada.reshape(DEPTH, 1, 3 * D_MODEL))


def _modulated_norm(x, mod_ref, nw_ref):
    ms = jnp.mean(x * x, axis=-1, keepdims=True)
    xn = x * lax.rsqrt(ms + EPS) * nw_ref[...]
    shift = mod_ref[0, :, 0:D_MODEL]
    scale = mod_ref[0, :, D_MODEL:2 * D_MODEL]
    return xn * (1.0 + scale) + shift


def _inproj_kernel(*refs, rope):
    if rope:
        (x_ref, mod_ref, nw_ref, w_ref, hm_ref, qw_ref, kw_ref, rc_ref, rs1_ref, rs2_ref,
         pc_ref, gc_ref, q_ref, k_ref, v_ref, gn_ref, qg_ref, kg_ref, vg_ref, gg_ref, lr_ref) = refs
    else:
        (x_ref, mod_ref, nw_ref, w_ref, hm_ref, qw_ref, kw_ref,
         pc_ref, gc_ref, q_ref, k_ref, v_ref, gn_ref, qg_ref, kg_ref, vg_ref, gg_ref, lr_ref) = refs
    h = _modulated_norm(x_ref[...], mod_ref, nw_ref).astype(BF16)

    def proj(c0, n):
        return _dot(h, w_ref[:, c0:c0 + n])

    pc_ref[...] = (proj(C_CA, BRANCH_W) * proj(C_XA, BRANCH_W)).astype(pc_ref.dtype)
    gc_ref[...] = (proj(C_BA, BRANCH_W) * _silu(proj(C_GA, BRANCH_W))).astype(gc_ref.dtype)

    def head_norm(u, g_ref):
        ms = _dot((u * u).astype(BF16), hm_ref[...])
        return u * lax.rsqrt(ms + EPS) * g_ref[...]

    q_ref[...] = (head_norm(proj(C_QN, BRANCH_W), qw_ref) * (NA_HD ** -0.5 * LOG2E)).astype(q_ref.dtype)
    k_ref[...] = head_norm(proj(C_KN, BRANCH_W), kw_ref).astype(k_ref.dtype)
    v_ref[...] = proj(C_VN, BRANCH_W).astype(v_ref.dtype)
    gn_ref[...] = _silu(proj(C_GN, BRANCH_W)).astype(gn_ref.dtype)

    qg = proj(C_QG, GLA_KW) * (GLA_DK ** -0.5)
    kg = proj(C_KG, GLA_KW)
    if rope:
        def rot(u):
            parts = []
            for j in range(GLA_KW // LANES):
                uj = u[:, j * LANES:(j + 1) * LANES]
                parts.append(uj * rc_ref[...]
                             + pltpu.roll(uj, LANES - GLA_DK // 4, axis=1) * rs1_ref[...]
                             + pltpu.roll(uj, GLA_DK // 4, axis=1) * rs2_ref[...])
            return jnp.concatenate(parts, axis=1)
        qg = rot(qg)
        kg = rot(kg)
    qg_ref[...] = qg.astype(qg_ref.dtype)
    kg_ref[...] = kg.astype(kg_ref.dtype)
    vg_ref[...] = proj(C_VG, BRANCH_W).astype(vg_ref.dtype)
    gg_ref[...] = _silu(proj(C_GG, BRANCH_W)).astype(gg_ref.dtype)
    lr_ref[...] = proj(C_LR, LANES)


def _inproj(x, mod, nw, w_in, hm, qw, kw, rope_tabs, seq, kv_dtype):
    n_tok = x.shape[0]
    tm = min(TOKEN_TILE, seq)
    tiles_per_seq = seq // tm
    rope = rope_tabs is not None

    def tok(w):
        return pl.BlockSpec((tm, w), lambda i: (i, 0))

    def const(shape):
        return pl.BlockSpec(shape, lambda i: (0,) * len(shape))

    in_specs = [
        tok(D_MODEL),
        pl.BlockSpec((1, 1, 3 * D_MODEL), lambda i: (i // tiles_per_seq, 0, 0)),
        const((1, D_MODEL)),
        const((D_MODEL, D_IN_PAD)),
        const((BRANCH_W, BRANCH_W)),
        const((1, BRANCH_W)),
        const((1, BRANCH_W)),
    ]
    args = [x, mod, nw, w_in, hm, qw, kw]
    if rope:
        in_specs += [pl.BlockSpec((tm, LANES), lambda i: (i % tiles_per_seq, 0))] * 3
        args += list(rope_tabs)
    widths = [BRANCH_W, BRANCH_W, BRANCH_W, BRANCH_W, BRANCH_W, BRANCH_W,
              GLA_KW, GLA_KW, BRANCH_W, BRANCH_W, LANES]
    dtypes = [BF16, BF16, BF16, kv_dtype, kv_dtype, BF16, BF16, BF16, BF16, BF16, F32]
    return pl.pallas_call(
        functools.partial(_inproj_kernel, rope=rope),
        grid=(n_tok // tm,),
        in_specs=in_specs,
        out_specs=[tok(w) for w in widths],
        out_shape=[jax.ShapeDtypeStruct((n_tok, w), d) for w, d in zip(widths, dtypes)],
        compiler_params=_cparams(1),
        name="inproj_rope" if rope else "inproj",
    )(*args)


def _pair_masks(dtype):
    lane = lax.broadcasted_iota(jnp.int32, (1, LANES), 1)
    return lane < NA_HD, lane >= NA_HD


def _softmax_pv(scores, values, head_lanes):
    m = functools.reduce(jnp.maximum, [jnp.max(s, axis=-1, keepdims=True) for s in scores])
    o = None
    for s, v in zip(scores, values):
        p = jnp.exp2(s - m).astype(BF16)
        t = _dot(p, jnp.where(head_lanes, v, jnp.ones_like(v)))
        o = t if o is None else o + t
    return o / pltpu.roll(o, NA_HD, axis=1)


def _na_ctx_kernel(q_ref, k_ref, v_ref, gn_ref, o_ref):
    m0, m1 = _pair_masks(BF16)
    for hp in range(NA_HEADS // 2):
        cols = slice(hp * LANES, (hp + 1) * LANES)
        qp = q_ref[0, :, cols]
        kp = k_ref[0, :, cols].astype(BF16)
        vp = v_ref[0, :, cols].astype(BF16)
        outs = []
        for msk in (m0, m1):
            qa = jnp.where(msk, qp, jnp.zeros_like(qp))
            outs.append(_softmax_pv([_dot_nt(qa, kp)], [vp], msk))
        o = jnp.where(m0, outs[0], outs[1])
        o_ref[0, :, cols] = (o * gn_ref[0, :, cols].astype(F32)).astype(o_ref.dtype)


def _na_ctx(q, k, v, gn, batch, seq):
    def blk():
        return pl.BlockSpec((1, seq, BRANCH_W), lambda b: (b, 0, 0))
    r3 = lambda u: u.reshape(batch, seq, BRANCH_W)
    out = pl.pallas_call(
        _na_ctx_kernel,
        grid=(batch,),
        in_specs=[blk(), blk(), blk(), blk()],
        out_specs=blk(),
        out_shape=jax.ShapeDtypeStruct((batch, seq, BRANCH_W), BF16),
        compiler_params=_cparams(1),
        name="na_ctx",
    )(r3(q), r3(k), r3(v), r3(gn))
    return out.reshape(batch * seq, BRANCH_W)


def _na_window_start(j, rows):
    return jnp.clip(j * NA_QROWS - WIN_R // 2, 0, rows - NA_KROWS)


def _na_lat_kernel(q_ref, k_ref, v_ref, kc_ref, vc_ref, bias_ref, gn_ref, o_ref, *, rows):
    j = pl.program_id(1)
    start = pl.multiple_of(_na_window_start(j, rows) * GRID_W, NA_QROWS * GRID_W)
    m0, m1 = _pair_masks(BF16)
    for hp in range(NA_HEADS // 2):
        cols = slice(hp * LANES, (hp + 1) * LANES)
        qp = q_ref[0, :, cols]
        kw = k_ref[0, pl.ds(start, NA_KROWS * GRID_W), cols]
        vw = v_ref[0, pl.ds(start, NA_KROWS * GRID_W), cols]
        kc = kc_ref[0, :, cols]
        vc = vc_ref[0, :, cols]
        outs = []
        for a, msk in enumerate((m0, m1)):
            qa = jnp.where(msk, qp, jnp.zeros_like(qp))
            s_w = _dot_nt(qa, kw) + bias_ref[0, 2 * hp + a].astype(F32)
            s_c = _dot_nt(qa, kc)
            outs.append(_softmax_pv([s_w, s_c], [vw, vc], msk))
        o = jnp.where(m0, outs[0], outs[1])
        o_ref[0, :, cols] = (o * gn_ref[0, :, cols].astype(F32)).astype(o_ref.dtype)


def _na_lat(q, k, v, kc, vc, bias, gn, batch, seq):
    rows = seq // GRID_W
    nj = rows // NA_QROWS
    tq = NA_QROWS * GRID_W

    def bias_type(j):
        return jnp.where(j == 0, 0, jnp.where(j == nj - 1, 2, 1))

    qblk = pl.BlockSpec((1, tq, BRANCH_W), lambda b, j: (b, j, 0))
    seqblk = pl.BlockSpec((1, seq, BRANCH_W), lambda b, j: (b, 0, 0))
    ctxblk = pl.BlockSpec((1, kc.shape[1], BRANCH_W), lambda b, j: (b, 0, 0))
    r3 = lambda u: u.reshape(batch, seq, BRANCH_W)
    out = pl.pallas_call(
        functools.partial(_na_lat_kernel, rows=rows),
        grid=(batch, nj),
        in_specs=[qblk, seqblk, seqblk, ctxblk, ctxblk,
                  pl.BlockSpec((1, NA_HEADS, tq, NA_KROWS * GRID_W),
                               lambda b, j: (bias_type(j), 0, 0, 0)),
                  qblk],
        out_specs=qblk,
        out_shape=jax.ShapeDtypeStruct((batch, seq, BRANCH_W), BF16),
        compiler_params=_cparams(2),
        name="na_lat",
    )(r3(q), r3(k), r3(v), kc, vc, bias, r3(gn))
    return out.reshape(batch * seq, BRANCH_W)


def _na_bias_kernel(rpb_ref, o_ref, *, rows):
    nj = rows // NA_QROWS
    qc = lax.broadcasted_iota(jnp.int32, (GRID_W, LANES), 0)
    lane = lax.broadcasted_iota(jnp.int32, (GRID_W, LANES), 1)
    kc = lane & (GRID_W - 1)
    cs = jnp.clip(qc - WIN_C // 2, 0, GRID_W - WIN_C)
    col_ok = (kc >= cs) & (kc < cs + WIN_C)
    left = lane < GRID_W
    neg = jnp.full((GRID_W, LANES), NEG_INF, F32)
    tl, tr = [], []
    for d in range(2 * WIN_R - 1):
        x = jnp.broadcast_to(rpb_ref[0, d:d + 1, :] * LOG2E, (GRID_W, LANES))
        tl.append(pltpu.roll(x, LANES - (WIN_C - 1), 1, stride=1, stride_axis=0))
        tr.append(pltpu.roll(x, GRID_W - (WIN_C - 1), 1, stride=1, stride_axis=0))
    for t, j in enumerate((0, 1, nj - 1)):
        start = int(np.clip(j * NA_QROWS - WIN_R // 2, 0, rows - NA_KROWS))
        for rl in range(NA_QROWS):
            r = j * NA_QROWS + rl
            rs = int(np.clip(r - WIN_R // 2, 0, rows - WIN_R))
            for kp in range(NA_KROWS // 2):
                k0 = start + 2 * kp
                a = tl[k0 - r + WIN_R - 1] if rs <= k0 < rs + WIN_R else neg
                b = tr[k0 + 1 - r + WIN_R - 1] if rs <= k0 + 1 < rs + WIN_R else neg
                blk = jnp.where(col_ok, jnp.where(left, a, b), NEG_INF)
                o_ref[t, 0, rl * GRID_W:(rl + 1) * GRID_W, kp * LANES:(kp + 1) * LANES] = (
                    blk.astype(o_ref.dtype))


def _na_bias_table(rpb, rows):
    rp = jnp.pad(rpb, ((0, 0), (0, 0), (0, LANES - (2 * WIN_C - 1))))
    tq, tk = NA_QROWS * GRID_W, NA_KROWS * GRID_W
    return pl.pallas_call(
        functools.partial(_na_bias_kernel, rows=rows),
        grid=(NA_HEADS,),
        in_specs=[pl.BlockSpec((1, 2 * WIN_R - 1, LANES), lambda h: (h, 0, 0))],
        out_specs=pl.BlockSpec((3, 1, tq, tk), lambda h: (0, h, 0, 0)),
        out_shape=jax.ShapeDtypeStruct((3, NA_HEADS, tq, tk), BF16),
        compiler_params=_cparams(1),
        name="na_bias",
    )(rp)


def _gla_kernel(q_ref, k_ref, v_ref, gg_ref, lr_ref, wa_ref, ba_ref, gw_ref, bd_ref, pm_ref,
                s0f_ref, s0b_ref, y_ref, sf_ref, sb_ref,
                of_ref, ob_ref, stf_ref, stb_ref, qi_ref, ku_ref, dec_ref, *, seq):
    c = GLA_CHUNK
    n = seq // c
    t = GLA_TILE
    lane = lax.broadcasted_iota(jnp.int32, (1, GLA_KW), 1)
    head_masks = [(lane >= h * GLA_DK) & (lane < (h + 1) * GLA_DK) for h in range(GLA_HEADS)]
    row = lax.broadcasted_iota(jnp.int32, (t, t), 0)
    col = lax.broadcasted_iota(jnp.int32, (t, t), 1)
    same_chunk = (row // c) == (col // c)
    causal = (same_chunk & (row >= col), same_chunk & (row <= col))
    wa = wa_ref[...].astype(BF16)
    outs = (of_ref, ob_ref)

    def chunk_rows(x, r):
        return jnp.concatenate(
            [jnp.broadcast_to(x[j * c + r:j * c + r + 1], (c, GLA_KW)) for j in range(t // c)], axis=0)

    def prep(i, carry):
        sl = pl.ds(pl.multiple_of(i * t, t), t)
        lr = lr_ref[0, sl, :].astype(BF16)
        q = q_ref[0, sl, :].astype(F32)
        k = k_ref[0, sl, :].astype(F32)
        v = v_ref[0, sl, :]
        for d in range(2):
            cols = slice(d * GLA_KW, (d + 1) * GLA_KW)
            la = _log_sigmoid(_dot(lr, wa[:, cols]) + ba_ref[:, cols]) * (1.0 / GLA_TAU)
            la_hi = la.astype(BF16)
            la_lo = (la - la_hi.astype(F32)).astype(BF16)
            g_cum = _dot(pm_ref[d], la_hi) + _dot(pm_ref[d], la_lo)
            g_all = chunk_rows(g_cum, (c - 1, 0)[d])
            g_ref = g_cum - chunk_rows(g_cum, (c // 2 - 1, c // 2)[d])
            qd = (q * jnp.exp(g_ref)).astype(BF16)
            kd = (k * jnp.exp(-g_ref)).astype(BF16)
            qi_ref[d, sl, :] = (q * jnp.exp(g_cum)).astype(BF16)
            ku_ref[d, sl, :] = (k * jnp.exp(g_all - g_cum)).astype(BF16)
            dec_ref[d, sl, :] = jnp.exp(g_all)
            parts = []
            for h in range(GLA_HEADS):
                qh = jnp.where(head_masks[h], qd, jnp.zeros_like(qd))
                att = jnp.where(causal[d], _dot_nt(qh, kd), 0.0)
                parts.append(_dot(att.astype(BF16), v[:, h * GLA_DV:(h + 1) * GLA_DV]))
            outs[d][sl, :] = jnp.concatenate(parts, axis=1)
        return carry

    lax.fori_loop(0, seq // t, prep, 0)

    stf_ref[...] = s0f_ref[0]
    stb_ref[...] = s0b_ref[0]

    def step(d, r0, st_ref):
        sl = pl.ds(r0, c)
        st = st_ref[...]
        outs[d][sl, :] += _dot_nt(qi_ref[d, sl, :], st.astype(BF16))
        upd = _dot_tn(v_ref[0, sl, :], ku_ref[d, sl, :])
        st_ref[...] = st * dec_ref[d, pl.ds(r0, 1), :] + upd * bd_ref[...]

    def scan(i, carry):
        step(0, pl.multiple_of(i * c, c), stf_ref)
        step(1, pl.multiple_of((n - 1 - i) * c, c), stb_ref)
        return carry

    lax.fori_loop(0, n, scan, 0)
    sf_ref[0] = stf_ref[...]
    sb_ref[0] = stb_ref[...]

    def finish(i, carry):
        sl = pl.ds(pl.multiple_of(i * t, t), t)
        o = of_ref[sl, :] + ob_ref[sl, :]
        parts = []
        for h in range(GLA_HEADS):
            oh = o[:, h * GLA_DV:(h + 1) * GLA_DV]
            ms = jnp.mean(oh * oh, axis=-1, keepdims=True)
            parts.append(oh * lax.rsqrt(ms + EPS))
        y = jnp.concatenate(parts, axis=1) * gw_ref[...]
        y_ref[0, sl, :] = (y * gg_ref[0, sl, :].astype(F32)).astype(y_ref.dtype)
        return carry

    lax.fori_loop(0, seq // t, finish, 0)


def _gla_sum_matrices():
    t, c = GLA_TILE, GLA_CHUNK
    r = np.arange(t)[:, None]
    j = np.arange(t)[None, :]
    same = (r // c) == (j // c)
    mats = [(same & (j <= r)).astype(np.float32), (same & (j >= r)).astype(np.float32)]
    return jnp.asarray(np.stack(mats), BF16)


def _gla(qg, kg, vg, gg, lr, wa, ba, gw, bd, pm, s0f, s0b, batch, seq):
    hv, hk = GLA_HEADS * GLA_DV, GLA_KW
    assert seq % GLA_TILE == 0

    def seqblk(w):
        return pl.BlockSpec((1, seq, w), lambda b: (b, 0, 0))

    def const(shape):
        return pl.BlockSpec(shape, lambda b: (0,) * len(shape))

    stblk = pl.BlockSpec((1, hv, hk), lambda b: (b, 0, 0))
    r3 = lambda u: u.reshape(batch, seq, u.shape[-1])
    y, sf, sb = pl.pallas_call(
        functools.partial(_gla_kernel, seq=seq),
        grid=(batch,),
        in_specs=[seqblk(hk), seqblk(hk), seqblk(hv), seqblk(hv), seqblk(LANES),
                  const((LANES, 2 * hk)), const((1, 2 * hk)), const((1, hv)), const((hv, hk)),
                  const((2, GLA_TILE, GLA_TILE)), stblk, stblk],
        out_specs=[seqblk(hv), stblk, stblk],
        out_shape=[jax.ShapeDtypeStruct((batch, seq, hv), BF16),
                   jax.ShapeDtypeStruct((batch, hv, hk), F32),
                   jax.ShapeDtypeStruct((batch, hv, hk), F32)],
        scratch_shapes=[pltpu.VMEM((seq, hv), F32), pltpu.VMEM((seq, hv), F32),
                        pltpu.VMEM((hv, hk), F32), pltpu.VMEM((hv, hk), F32),
                        pltpu.VMEM((2, seq, hk), BF16), pltpu.VMEM((2, seq, hk), BF16),
                        pltpu.VMEM((2, seq, hk), F32)],
        compiler_params=_cparams(1),
        name="gla",
    )(r3(qg), r3(kg), r3(vg), r3(gg), r3(lr), wa, ba, gw, bd, pm, s0f, s0b)
    return y.reshape(batch * seq, hv), sf, sb


def _state_to_blockdiag_t(s):
    b = s.shape[0]
    eye = jnp.eye(GLA_HEADS, dtype=s.dtype)
    t = jnp.einsum("bhkv,hg->bhvgk", s, eye)
    return t.reshape(b, GLA_HEADS * GLA_DV, GLA_KW)


def _blockdiag_t_to_state(t):
    b = t.shape[0]
    t5 = t.reshape(b, GLA_HEADS, GLA_DV, GLA_HEADS, GLA_DK)
    d = jnp.stack([t5[:, h, :, h, :] for h in range(GLA_HEADS)], axis=1)
    return d.transpose(0, 1, 3, 2)


def _merge_kernel(x_ref, mod_ref, nw_ref, pc_ref, pprev_ref, pnext_ref, gc_ref, cw_ref,
                  yna_ref, ygla_ref, wg_ref, bg_ref, wb_ref, wo_ref, o_ref, *, tiles_per_seq):
    i = pl.program_id(0)
    x = x_ref[...]
    h = _modulated_norm(x, mod_ref, nw_ref).astype(BF16)
    tm = x.shape[0]

    pc = pc_ref[...].astype(F32)
    pos = i % tiles_per_seq
    prev_row = jnp.where(pos == 0, 0.0, pprev_ref[HALO - 1:HALO, :].astype(F32))
    next_row = jnp.where(pos == tiles_per_seq - 1, 0.0, pnext_ref[0:1, :].astype(F32))
    ridx = lax.broadcasted_iota(jnp.int32, (tm, 1), 0)
    before = jnp.where(ridx == 0, prev_row, pltpu.roll(pc, 1, axis=0))
    after = jnp.where(ridx == tm - 1, next_row, pltpu.roll(pc, tm - 1, axis=0))
    conv = before * cw_ref[0:1, :] + pc * cw_ref[1:2, :] + after * cw_ref[2:3, :]
    y_conv = (gc_ref[...].astype(F32) * conv).astype(BF16)

    merged = None
    for b, yb in enumerate((y_conv, yna_ref[...], ygla_ref[...])):
        cols = slice(b * D_MODEL, (b + 1) * D_MODEL)
        g = _sigmoid(_dot(h, wg_ref[:, cols]) + bg_ref[:, cols])
        t = g * _dot(yb, wb_ref[b])
        merged = t if merged is None else merged + t
    gate = mod_ref[0, :, 2 * D_MODEL:3 * D_MODEL]
    o_ref[...] = x + gate * _dot(merged.astype(BF16), wo_ref[...])


def _merge(x, mod, nw, pc, gc, cw, yna, ygla, wg, bg, wb, wo, seq):
    n_tok = x.shape[0]
    tm = min(TOKEN_TILE, seq)
    tiles_per_seq = seq // tm
    hb = tm // HALO
    last = n_tok // HALO - 1

    def tok(w):
        return pl.BlockSpec((tm, w), lambda i: (i, 0))

    def const(shape):
        return pl.BlockSpec(shape, lambda i: (0,) * len(shape))

    return pl.pallas_call(
        functools.partial(_merge_kernel, tiles_per_seq=tiles_per_seq),
        grid=(n_tok // tm,),
        in_specs=[
            tok(D_MODEL),
            pl.BlockSpec((1, 1, 3 * D_MODEL), lambda i: (i // tiles_per_seq, 0, 0)),
            const((1, D_MODEL)),
            tok(BRANCH_W),
            pl.BlockSpec((HALO, BRANCH_W), lambda i: (jnp.maximum(i * hb - 1, 0), 0)),
            pl.BlockSpec((HALO, BRANCH_W), lambda i: (jnp.minimum((i + 1) * hb, last), 0)),
            tok(BRANCH_W),
            const((3, BRANCH_W)),
            tok(BRANCH_W),
            tok(BRANCH_W),
            const((D_MODEL, N_BRANCH * D_MODEL)),
            const((1, N_BRANCH * D_MODEL)),
            const((N_BRANCH, BRANCH_W, D_MODEL)),
            const((D_MODEL, D_MODEL)),
        ],
        out_specs=tok(D_MODEL),
        out_shape=jax.ShapeDtypeStruct((n_tok, D_MODEL), F32),
        compiler_params=_cparams(1),
        name="merge",
    )(x, mod, nw, pc, pc, pc, gc, cw, yna, ygla, wg, bg, wb, wo)


def _rope_tables(seq):
    pos = np.arange(seq)
    n_f = GLA_DK // 4
    inv = ROPE_BASE ** (-np.arange(n_f) / n_f)
    ang_r = ((pos // GRID_W)[:, None] * inv).astype(np.float32).astype(np.float64)
    ang_c = ((pos % GRID_W)[:, None] * inv).astype(np.float32).astype(np.float64)
    zero = np.zeros_like(ang_r)
    cos = np.concatenate([np.cos(ang_r), np.cos(ang_r), np.cos(ang_c), np.cos(ang_c)], axis=1)
    s1 = np.concatenate([-np.sin(ang_r), zero, -np.sin(ang_c), zero], axis=1)
    s2 = np.concatenate([zero, np.sin(ang_r), zero, np.sin(ang_c)], axis=1)
    tile = lambda t: jnp.asarray(np.tile(t, (1, LANES // GLA_DK)).astype(np.float32))
    return tile(cos), tile(s1), tile(s2)


def _head_mean_matrix():
    h = np.arange(BRANCH_W) // NA_HD
    return jnp.asarray((h[:, None] == h[None, :]).astype(np.float32) / NA_HD, BF16)


def _gla_blockdiag_mask():
    r = np.arange(GLA_HEADS * GLA_DV) // GLA_DV
    c = np.arange(GLA_KW) // GLA_DK
    return jnp.asarray((r[:, None] == c[None, :]).astype(np.float32), F32)


def kernel(x_prompt, x_sample, c, cache_k, cache_v, state_fwd, state_bwd, c_ctx,
           norm_w, w_ada, b_ada, w_in, conv_w, q_norm_w, k_norm_w, rpb,
           w_alpha, b_alpha, gla_norm_w, w_branch, w_gate, b_gate, w_out):
    nb_p, seq_p, _ = x_prompt.shape
    nb_s, seq_s, _ = x_sample.shape
    past = cache_k.shape[3]

    cond = jnp.concatenate([c, c_ctx[None, :], jnp.zeros((16 - nb_s - 1, D_MODEL), F32)], axis=0)
    mods = _adaln(cond, w_ada, b_ada)

    w_in_b = jnp.pad(w_in, ((0, 0), (0, 0), (0, D_IN_PAD - D_IN))).astype(BF16)
    w_gate_b = w_gate.astype(BF16)
    w_branch_b = w_branch.astype(BF16)
    w_out_b = w_out.astype(BF16)
    hm = _head_mean_matrix()
    bd = _gla_blockdiag_mask()
    pm = _gla_sum_matrices()
    rope_tabs = _rope_tables(seq_s)

    ck = cache_k.transpose(0, 1, 3, 2, 4).reshape(nb_s, DEPTH, past, BRANCH_W).astype(BF16)
    cv = cache_v.transpose(0, 1, 3, 2, 4).reshape(nb_s, DEPTH, past, BRANCH_W).astype(BF16)

    y_p = x_prompt.reshape(nb_p * seq_p, D_MODEL)
    y_s = x_sample.reshape(nb_s * seq_s, D_MODEL)
    zeros_state = jnp.zeros((nb_p, GLA_HEADS * GLA_DV, GLA_KW), F32)
    ks, vs, sfs, sbs = [], [], [], []
    for l in range(DEPTH):
        nw = norm_w[l].reshape(1, D_MODEL)
        qw = jnp.tile(q_norm_w[l], NA_HEADS).reshape(1, BRANCH_W)
        kw = jnp.tile(k_norm_w[l], NA_HEADS).reshape(1, BRANCH_W)
        gw = jnp.tile(gla_norm_w[l], GLA_HEADS).reshape(1, BRANCH_W)
        wa = jnp.zeros((LANES, 2 * GLA_KW), F32)
        wa = wa.at[0:GLA_RANK, 0:GLA_KW].set(w_alpha[l, 0])
        wa = wa.at[GLA_RANK:2 * GLA_RANK, GLA_KW:].set(w_alpha[l, 1])
        ba = b_alpha[l].reshape(1, 2 * GLA_KW)
        bg = b_gate[l].reshape(1, N_BRANCH * D_MODEL)

        mod_p = mods[l, nb_s:nb_s + 1].reshape(1, 1, 3 * D_MODEL)
        mod_p = jnp.broadcast_to(mod_p, (nb_p, 1, 3 * D_MODEL))
        (pc, gc, q, k, v, gn, qg, kg, vg, gg, lr) = _inproj(
            y_p, mod_p, nw, w_in_b[l], hm, qw, kw, None, seq_p, F32)
        yna = _na_ctx(q, k, v, gn, nb_p, seq_p)
        ygla, sf, sb = _gla(qg, kg, vg, gg, lr, wa, ba, gw, bd, pm, zeros_state, zeros_state,
                            nb_p, seq_p)
        y_p = _merge(y_p, mod_p, nw, pc, gc, conv_w[l], yna, ygla,
                     w_gate_b[l], bg, w_branch_b[l], w_out_b[l], seq_p)
        ks.append(k.reshape(nb_p, seq_p, NA_HEADS, NA_HD).transpose(0, 2, 1, 3))
        vs.append(v.reshape(nb_p, seq_p, NA_HEADS, NA_HD).transpose(0, 2, 1, 3))
        sfs.append(_blockdiag_t_to_state(sf))
        sbs.append(_blockdiag_t_to_state(sb))

        mod_s = mods[l, 0:nb_s].reshape(nb_s, 1, 3 * D_MODEL)
        (pc, gc, q, k, v, gn, qg, kg, vg, gg, lr) = _inproj(
            y_s, mod_s, nw, w_in_b[l], hm, qw, kw, rope_tabs, seq_s, BF16)
        bias = _na_bias_table(rpb[l], seq_s // GRID_W)
        yna = _na_lat(q, k, v, ck[:, l], cv[:, l], bias, gn, nb_s, seq_s)
        ygla, _, _ = _gla(qg, kg, vg, gg, lr, wa, ba, gw, bd, pm,
                          _state_to_blockdiag_t(state_fwd[:, l]),
                          _state_to_blockdiag_t(state_bwd[:, l]), nb_s, seq_s)
        y_s = _merge(y_s, mod_s, nw, pc, gc, conv_w[l], yna, ygla,
                     w_gate_b[l], bg, w_branch_b[l], w_out_b[l], seq_s)

    return (y_p.reshape(nb_p, seq_p, D_MODEL), y_s.reshape(nb_s, seq_s, D_MODEL),
            jnp.stack(ks, axis=1), jnp.stack(vs, axis=1),
            jnp.stack(sfs, axis=1), jnp.stack(sbs, axis=1))
```

```python
import functools

import numpy as np
import jax
import jax.numpy as jnp
from jax import lax
from jax.experimental import pallas as pl
from jax.experimental.pallas import tpu as pltpu

F32 = jnp.float32
BF16 = jnp.bfloat16

D_MODEL = 1024
DEPTH = 4
GRID_W = 64
EPS = 1e-6
NEG_INF = -1e30
BRANCH_W = D_MODEL // 2
N_BRANCH = 3
NA_HEADS = 8
NA_HD = BRANCH_W // NA_HEADS
WIN_R = 8
WIN_C = 16
GLA_HEADS = 4
GLA_DV = BRANCH_W // GLA_HEADS
GLA_DK = GLA_DV // 2
GLA_KW = GLA_HEADS * GLA_DK
GLA_RANK = 16
GLA_TAU = 16.0
ROPE_BASE = 10000.0
LOG2E = 1.4426950408889634

LANES = 128
D_IN = 8 * BRANCH_W + 2 * GLA_KW + 2 * BRANCH_W + 2 * GLA_RANK
D_IN_PAD = ((D_IN + LANES - 1) // LANES) * LANES
C_XA, C_BA, C_CA, C_GA = 0, 512, 1024, 1536
C_QN, C_KN, C_VN, C_GN = 2048, 2560, 3072, 3584
C_QG, C_KG, C_VG, C_GG, C_LR = 4096, 4352, 4608, 5120, 5632

VMEM_LIMIT = 56 * 1024 * 1024
TOKEN_TILE = 1024
GLA_CHUNK = 64
GLA_TILE = 256
NA_QROWS = 4
NA_KROWS = 12
HALO = 16


def _cparams(n_axes):
    return pltpu.CompilerParams(dimension_semantics=("arbitrary",) * n_axes,
                                vmem_limit_bytes=VMEM_LIMIT)


def _resident(shape):
    return pl.BlockSpec(shape, lambda i: (0,) * len(shape), pipeline_mode=pl.Buffered(1))


def _sigmoid(x):
    return 1.0 / (1.0 + jnp.exp(-x))


def _silu(x):
    return x * _sigmoid(x)


def _log_sigmoid(x):
    return jnp.minimum(x, 0.0) - jnp.log(1.0 + jnp.exp(-jnp.abs(x)))


def _dot(a, b):
    return jnp.dot(a, b, preferred_element_type=F32)


def _dot_nt(a, b):
    return lax.dot_general(a, b, (((1,), (1,)), ((), ())), preferred_element_type=F32)


def _dot_tn(a, b):
    return lax.dot_general(a, b, (((0,), (0,)), ((), ())), preferred_element_type=F32)


def _ada_kernel(cond_ref, w_ref, b_ref, o_ref):
    c = cond_ref[...]
    o_ref[0] = _dot(_silu(c).astype(BF16), w_ref[0].astype(BF16)) + b_ref[0]


def _adaln(cond, w_ada, b_ada):
    r = cond.shape[0]
    tn = 768
    return pl.pallas_call(
        _ada_kernel,
        grid=(DEPTH, 3 * D_MODEL // tn),
        in_specs=[
            pl.BlockSpec((r, D_MODEL), lambda l, n: (0, 0)),
            pl.BlockSpec((1, D_MODEL, tn), lambda l, n: (l, 0, n)),
            pl.BlockSpec((1, 1, tn), lambda l, n: (l, 0, n)),
        ],
        out_specs=pl.BlockSpec((1, r, tn), lambda l, n: (l, 0, n)),
        out_shape=jax.ShapeDtypeStruct((DEPTH, r, 3 * D_MODEL), F32),
        compiler_params=_cparams(2),
        name="adaln",
    )(cond, w_ada, b_ada.reshape(DEPTH, 1, 3 * D_MODEL))


def _modulated_norm(x, mod_ref, nw_ref):
    ms = jnp.mean(x * x, axis=-1, keepdims=True)
    xn = x * lax.rsqrt(ms + EPS) * nw_ref[...]
    shift = mod_ref[0, :, 0:D_MODEL]
    scale = mod_ref[0, :, D_MODEL:2 * D_MODEL]
    return xn * (1.0 + scale) + shift


def _inproj_kernel(*refs, rope):
    if rope:
        (x_ref, mod_ref, nw_ref, w_ref, hm_ref, qw_ref, kw_ref, rc_ref, rs1_ref, rs2_ref,
         pc_ref, gc_ref, q_ref, k_ref, v_ref, gn_ref, qg_ref, kg_ref, vg_ref, gg_ref, lr_ref) = refs
    else:
        (x_ref, mod_ref, nw_ref, w_ref, hm_ref, qw_ref, kw_ref,
         pc_ref, gc_ref, q_ref, k_ref, v_ref, gn_ref, qg_ref, kg_ref, vg_ref, gg_ref, lr_ref) = refs
    h = _modulated_norm(x_ref[...], mod_ref, nw_ref).astype(BF16)

    def proj(c0, n):
        return _dot(h, w_ref[:, c0:c0 + n])

    pc_ref[...] = (proj(C_CA, BRANCH_W) * proj(C_XA, BRANCH_W)).astype(pc_ref.dtype)
    gc_ref[...] = (proj(C_BA, BRANCH_W) * _silu(proj(C_GA, BRANCH_W))).astype(gc_ref.dtype)

    def head_norm(u, g_ref):
        ms = _dot((u * u).astype(BF16), hm_ref[...])
        return u * lax.rsqrt(ms + EPS) * g_ref[...]

    q_ref[...] = (head_norm(proj(C_QN, BRANCH_W), qw_ref) * (NA_HD ** -0.5 * LOG2E)).astype(q_ref.dtype)
    k_ref[...] = head_norm(proj(C_KN, BRANCH_W), kw_ref).astype(k_ref.dtype)
    v_ref[...] = proj(C_VN, BRANCH_W).astype(v_ref.dtype)
    gn_ref[...] = _silu(proj(C_GN, BRANCH_W)).astype(gn_ref.dtype)

    qg = proj(C_QG, GLA_KW) * (GLA_DK ** -0.5)
    kg = proj(C_KG, GLA_KW)
    if rope:
        def rot(u):
            parts = []
            for j in range(GLA_KW // LANES):
                uj = u[:, j * LANES:(j + 1) * LANES]
                parts.append(uj * rc_ref[...]
                             + pltpu.roll(uj, LANES - GLA_DK // 4, axis=1) * rs1_ref[...]
                             + pltpu.roll(uj, GLA_DK // 4, axis=1) * rs2_ref[...])
            return jnp.concatenate(parts, axis=1)
        qg = rot(qg)
        kg = rot(kg)
    qg_ref[...] = qg.astype(qg_ref.dtype)
    kg_ref[...] = kg.astype(kg_ref.dtype)
    vg_ref[...] = proj(C_VG, BRANCH_W).astype(vg_ref.dtype)
    gg_ref[...] = _silu(proj(C_GG, BRANCH_W)).astype(gg_ref.dtype)
    lr_ref[...] = proj(C_LR, LANES)


def _inproj(x, mod, nw, w_in, hm, qw, kw, rope_tabs, seq, kv_dtype):
    n_tok = x.shape[0]
    tm = min(TOKEN_TILE, seq)
    tiles_per_seq = seq // tm
    rope = rope_tabs is not None

    def tok(w):
        return pl.BlockSpec((tm, w), lambda i: (i, 0))

    def const(shape):
        return pl.BlockSpec(shape, lambda i: (0,) * len(shape))

    in_specs = [
        tok(D_MODEL),
        pl.BlockSpec((1, 1, 3 * D_MODEL), lambda i: (i // tiles_per_seq, 0, 0)),
        const((1, D_MODEL)),
        _resident((D_MODEL, D_IN_PAD)),
        const((BRANCH_W, BRANCH_W)),
        const((1, BRANCH_W)),
        const((1, BRANCH_W)),
    ]
    args = [x, mod, nw, w_in, hm, qw, kw]
    if rope:
        in_specs += [pl.BlockSpec((tm, LANES), lambda i: (i % tiles_per_seq, 0))] * 3
        args += list(rope_tabs)
    widths = [BRANCH_W, BRANCH_W, BRANCH_W, BRANCH_W, BRANCH_W, BRANCH_W,
              GLA_KW, GLA_KW, BRANCH_W, BRANCH_W, LANES]
    dtypes = [BF16, BF16, BF16, kv_dtype, kv_dtype, BF16, BF16, BF16, BF16, BF16, F32]
    return pl.pallas_call(
        functools.partial(_inproj_kernel, rope=rope),
        grid=(n_tok // tm,),
        in_specs=in_specs,
        out_specs=[tok(w) for w in widths],
        out_shape=[jax.ShapeDtypeStruct((n_tok, w), d) for w, d in zip(widths, dtypes)],
        compiler_params=_cparams(1),
        name="inproj_rope" if rope else "inproj",
    )(*args)


def _pair_masks(dtype):
    lane = lax.broadcasted_iota(jnp.int32, (1, LANES), 1)
    return lane < NA_HD, lane >= NA_HD


def _softmax_pv(scores, values, head_lanes):
    m = functools.reduce(jnp.maximum, [jnp.max(s, axis=-1, keepdims=True) for s in scores])
    o = None
    for s, v in zip(scores, values):
        p = jnp.exp2(s - m).astype(BF16)
        t = _dot(p, jnp.where(head_lanes, v, jnp.ones_like(v)))
        o = t if o is None else o + t
    return o / pltpu.roll(o, NA_HD, axis=1)


def _na_ctx_kernel(q_ref, k_ref, v_ref, gn_ref, o_ref):
    m0, m1 = _pair_masks(BF16)
    for hp in range(NA_HEADS // 2):
        cols = slice(hp * LANES, (hp + 1) * LANES)
        qp = q_ref[0, :, cols]
        kp = k_ref[0, :, cols].astype(BF16)
        vp = v_ref[0, :, cols].astype(BF16)
        outs = []
        for msk in (m0, m1):
            qa = jnp.where(msk, qp, jnp.zeros_like(qp))
            outs.append(_softmax_pv([_dot_nt(qa, kp)], [vp], msk))
        o = jnp.where(m0, outs[0], outs[1])
        o_ref[0, :, cols] = (o * gn_ref[0, :, cols].astype(F32)).astype(o_ref.dtype)


def _na_ctx(q, k, v, gn, batch, seq):
    def blk():
        return pl.BlockSpec((1, seq, BRANCH_W), lambda b: (b, 0, 0))
    r3 = lambda u: u.reshape(batch, seq, BRANCH_W)
    out = pl.pallas_call(
        _na_ctx_kernel,
        grid=(batch,),
        in_specs=[blk(), blk(), blk(), blk()],
        out_specs=blk(),
        out_shape=jax.ShapeDtypeStruct((batch, seq, BRANCH_W), BF16),
        compiler_params=_cparams(1),
        name="na_ctx",
    )(r3(q), r3(k), r3(v), r3(gn))
    return out.reshape(batch * seq, BRANCH_W)


def _na_window_start(j, rows):
    return jnp.clip(j * NA_QROWS - WIN_R // 2, 0, rows - NA_KROWS)


def _na_lat_kernel(q_ref, k_ref, v_ref, kc_ref, vc_ref, bias_ref, gn_ref, o_ref, *, rows):
    j = pl.program_id(1)
    start = pl.multiple_of(_na_window_start(j, rows) * GRID_W, NA_QROWS * GRID_W)
    m0, m1 = _pair_masks(BF16)
    for hp in range(NA_HEADS // 2):
        cols = slice(hp * LANES, (hp + 1) * LANES)
        qp = q_ref[0, :, cols]
        kw = k_ref[0, pl.ds(start, NA_KROWS * GRID_W), cols]
        vw = v_ref[0, pl.ds(start, NA_KROWS * GRID_W), cols]
        kc = kc_ref[0, :, cols]
        vc = vc_ref[0, :, cols]
        tq = qp.shape[0]
        q2 = jnp.concatenate([jnp.where(m0, qp, jnp.zeros_like(qp)),
                              jnp.where(m1, qp, jnp.zeros_like(qp))], axis=0)
        bias = bias_ref[0, 2 * hp:2 * hp + 2].reshape(2 * tq, NA_KROWS * GRID_W)
        s_w = _dot_nt(q2, kw) + bias.astype(F32)
        s_c = _dot_nt(q2, kc)
        outs = []
        for a, msk in enumerate((m0, m1)):
            rows_a = slice(a * tq, (a + 1) * tq)
            outs.append(_softmax_pv([s_w[rows_a], s_c[rows_a]], [vw, vc], msk))
        o = jnp.where(m0, outs[0], outs[1])
        o_ref[0, :, cols] = (o * gn_ref[0, :, cols].astype(F32)).astype(o_ref.dtype)


def _na_lat(q, k, v, kc, vc, bias, gn, batch, seq):
    rows = seq // GRID_W
    nj = rows // NA_QROWS
    tq = NA_QROWS * GRID_W

    def bias_type(j):
        return jnp.where(j == 0, 0, jnp.where(j == nj - 1, 2, 1))

    qblk = pl.BlockSpec((1, tq, BRANCH_W), lambda b, j: (b, j, 0))
    seqblk = pl.BlockSpec((1, seq, BRANCH_W), lambda b, j: (b, 0, 0))
    ctxblk = pl.BlockSpec((1, kc.shape[1], BRANCH_W), lambda b, j: (b, 0, 0))
    r3 = lambda u: u.reshape(batch, seq, BRANCH_W)
    out = pl.pallas_call(
        functools.partial(_na_lat_kernel, rows=rows),
        grid=(batch, nj),
        in_specs=[qblk, seqblk, seqblk, ctxblk, ctxblk,
                  pl.BlockSpec((1, NA_HEADS, tq, NA_KROWS * GRID_W),
                               lambda b, j: (bias_type(j), 0, 0, 0)),
                  qblk],
        out_specs=qblk,
        out_shape=jax.ShapeDtypeStruct((batch, seq, BRANCH_W), BF16),
        compiler_params=_cparams(2),
        name="na_lat",
    )(r3(q), r3(k), r3(v), kc, vc, bias, r3(gn))
    return out.reshape(batch * seq, BRANCH_W)


def _na_bias_kernel(rpb_ref, o_ref, *, rows):
    nj = rows // NA_QROWS
    qc = lax.broadcasted_iota(jnp.int32, (GRID_W, LANES), 0)
    lane = lax.broadcasted_iota(jnp.int32, (GRID_W, LANES), 1)
    kc = lane & (GRID_W - 1)
    cs = jnp.clip(qc - WIN_C // 2, 0, GRID_W - WIN_C)
    col_ok = (kc >= cs) & (kc < cs + WIN_C)
    left = lane < GRID_W
    neg = jnp.full((GRID_W, LANES), NEG_INF, F32)
    tl, tr = [], []
    for d in range(2 * WIN_R - 1):
        x = jnp.broadcast_to(rpb_ref[0, d:d + 1, :] * LOG2E, (GRID_W, LANES))
        tl.append(pltpu.roll(x, LANES - (WIN_C - 1), 1, stride=1, stride_axis=0))
        tr.append(pltpu.roll(x, GRID_W - (WIN_C - 1), 1, stride=1, stride_axis=0))
    for t, j in enumerate((0, 1, nj - 1)):
        start = int(np.clip(j * NA_QROWS - WIN_R // 2, 0, rows - NA_KROWS))
        for rl in range(NA_QROWS):
            r = j * NA_QROWS + rl
            rs = int(np.clip(r - WIN_R // 2, 0, rows - WIN_R))
            for kp in range(NA_KROWS // 2):
                k0 = start + 2 * kp
                a = tl[k0 - r + WIN_R - 1] if rs <= k0 < rs + WIN_R else neg
                b = tr[k0 + 1 - r + WIN_R - 1] if rs <= k0 + 1 < rs + WIN_R else neg
                blk = jnp.where(col_ok, jnp.where(left, a, b), NEG_INF)
                o_ref[t, 0, rl * GRID_W:(rl + 1) * GRID_W, kp * LANES:(kp + 1) * LANES] = (
                    blk.astype(o_ref.dtype))


def _na_bias_table(rpb, rows):
    rp = jnp.pad(rpb, ((0, 0), (0, 0), (0, LANES - (2 * WIN_C - 1))))
    tq, tk = NA_QROWS * GRID_W, NA_KROWS * GRID_W
    return pl.pallas_call(
        functools.partial(_na_bias_kernel, rows=rows),
        grid=(NA_HEADS,),
        in_specs=[pl.BlockSpec((1, 2 * WIN_R - 1, LANES), lambda h: (h, 0, 0))],
        out_specs=pl.BlockSpec((3, 1, tq, tk), lambda h: (0, h, 0, 0)),
        out_shape=jax.ShapeDtypeStruct((3, NA_HEADS, tq, tk), BF16),
        compiler_params=_cparams(1),
        name="na_bias",
    )(rp)


def _gla_kernel(q_ref, k_ref, v_ref, gg_ref, lr_ref, wa_ref, ba_ref, gw_ref, bd_ref, pm_ref,
                s0f_ref, s0b_ref, y_ref, sf_ref, sb_ref,
                of_ref, ob_ref, stf_ref, stb_ref, qi_ref, ku_ref, dec_ref, *, seq):
    c = GLA_CHUNK
    n = seq // c
    t = GLA_TILE
    lane = lax.broadcasted_iota(jnp.int32, (1, GLA_KW), 1)
    head_masks = [(lane >= h * GLA_DK) & (lane < (h + 1) * GLA_DK) for h in range(GLA_HEADS)]
    row = lax.broadcasted_iota(jnp.int32, (t, t), 0)
    col = lax.broadcasted_iota(jnp.int32, (t, t), 1)
    same_chunk = (row // c) == (col // c)
    causal = (same_chunk & (row >= col), same_chunk & (row <= col))
    wa = wa_ref[...].astype(BF16)
    outs = (of_ref, ob_ref)

    def chunk_rows(x, r):
        return jnp.concatenate(
            [jnp.broadcast_to(x[j * c + r:j * c + r + 1], (c, GLA_KW)) for j in range(t // c)], axis=0)

    def prep(i, carry):
        sl = pl.ds(pl.multiple_of(i * t, t), t)
        lr = lr_ref[0, sl, :].astype(BF16)
        q = q_ref[0, sl, :].astype(F32)
        k = k_ref[0, sl, :].astype(F32)
        v = v_ref[0, sl, :]
        for d in range(2):
            cols = slice(d * GLA_KW, (d + 1) * GLA_KW)
            la = _log_sigmoid(_dot(lr, wa[:, cols]) + ba_ref[:, cols]) * (1.0 / GLA_TAU)
            la_hi = la.astype(BF16)
            la_lo = (la - la_hi.astype(F32)).astype(BF16)
            g_cum = _dot(pm_ref[d], la_hi) + _dot(pm_ref[d], la_lo)
            g_all = chunk_rows(g_cum, (c - 1, 0)[d])
            g_ref = g_cum - chunk_rows(g_cum, (c // 2 - 1, c // 2)[d])
            qd = (q * jnp.exp(g_ref)).astype(BF16)
            kd = (k * jnp.exp(-g_ref)).astype(BF16)
            qi_ref[d, sl, :] = (q * jnp.exp(g_cum)).astype(BF16)
            ku_ref[d, sl, :] = (k * jnp.exp(g_all - g_cum)).astype(BF16)
            dec_ref[d, sl, :] = jnp.exp(g_all)
            parts = []
            for h in range(GLA_HEADS):
                qh = jnp.where(head_masks[h], qd, jnp.zeros_like(qd))
                att = jnp.where(causal[d], _dot_nt(qh, kd), 0.0)
                parts.append(_dot(att.astype(BF16), v[:, h * GLA_DV:(h + 1) * GLA_DV]))
            outs[d][sl, :] = jnp.concatenate(parts, axis=1)
        return carry

    lax.fori_loop(0, seq // t, prep, 0)

    stf_ref[...] = s0f_ref[0]
    stb_ref[...] = s0b_ref[0]

    def step(d, r0, st_ref):
        sl = pl.ds(r0, c)
        st = st_ref[...]
        outs[d][sl, :] += _dot_nt(qi_ref[d, sl, :], st.astype(BF16))
        upd = _dot_tn(v_ref[0, sl, :], ku_ref[d, sl, :])
        st_ref[...] = st * dec_ref[d, pl.ds(r0, 1), :] + upd * bd_ref[...]

    def scan(i, carry):
        step(0, pl.multiple_of(i * c, c), stf_ref)
        step(1, pl.multiple_of((n - 1 - i) * c, c), stb_ref)
        return carry

    lax.fori_loop(0, n, scan, 0)
    sf_ref[0] = stf_ref[...]
    sb_ref[0] = stb_ref[...]

    def finish(i, carry):
        sl = pl.ds(pl.multiple_of(i * t, t), t)
        o = of_ref[sl, :] + ob_ref[sl, :]
        parts = []
        for h in range(GLA_HEADS):
            oh = o[:, h * GLA_DV:(h + 1) * GLA_DV]
            ms = jnp.mean(oh * oh, axis=-1, keepdims=True)
            parts.append(oh * lax.rsqrt(ms + EPS))
        y = jnp.concatenate(parts, axis=1) * gw_ref[...]
        y_ref[0, sl, :] = (y * gg_ref[0, sl, :].astype(F32)).astype(y_ref.dtype)
        return carry

    lax.fori_loop(0, seq // t, finish, 0)


def _gla_sum_matrices():
    t, c = GLA_TILE, GLA_CHUNK
    r = np.arange(t)[:, None]
    j = np.arange(t)[None, :]
    same = (r // c) == (j // c)
    mats = [(same & (j <= r)).astype(np.float32), (same & (j >= r)).astype(np.float32)]
    return jnp.asarray(np.stack(mats), BF16)


def _gla(qg, kg, vg, gg, lr, wa, ba, gw, bd, pm, s0f, s0b, batch, seq):
    hv, hk = GLA_HEADS * GLA_DV, GLA_KW
    assert seq % GLA_TILE == 0

    def seqblk(w):
        return pl.BlockSpec((1, seq, w), lambda b: (b, 0, 0))

    def const(shape):
        return pl.BlockSpec(shape, lambda b: (0,) * len(shape))

    stblk = pl.BlockSpec((1, hv, hk), lambda b: (b, 0, 0))
    r3 = lambda u: u.reshape(batch, seq, u.shape[-1])
    y, sf, sb = pl.pallas_call(
        functools.partial(_gla_kernel, seq=seq),
        grid=(batch,),
        in_specs=[seqblk(hk), seqblk(hk), seqblk(hv), seqblk(hv), seqblk(LANES),
                  const((LANES, 2 * hk)), const((1, 2 * hk)), const((1, hv)), const((hv, hk)),
                  const((2, GLA_TILE, GLA_TILE)), stblk, stblk],
        out_specs=[seqblk(hv), stblk, stblk],
        out_shape=[jax.ShapeDtypeStruct((batch, seq, hv), BF16),
                   jax.ShapeDtypeStruct((batch, hv, hk), F32),
                   jax.ShapeDtypeStruct((batch, hv, hk), F32)],
        scratch_shapes=[pltpu.VMEM((seq, hv), F32), pltpu.VMEM((seq, hv), F32),
                        pltpu.VMEM((hv, hk), F32), pltpu.VMEM((hv, hk), F32),
                        pltpu.VMEM((2, seq, hk), BF16), pltpu.VMEM((2, seq, hk), BF16),
                        pltpu.VMEM((2, seq, hk), F32)],
        compiler_params=_cparams(1),
        name="gla",
    )(r3(qg), r3(kg), r3(vg), r3(gg), r3(lr), wa, ba, gw, bd, pm, s0f, s0b)
    return y.reshape(batch * seq, hv), sf, sb


def _state_to_blockdiag_t(s):
    b = s.shape[0]
    eye = jnp.eye(GLA_HEADS, dtype=s.dtype)
    t = jnp.einsum("bhkv,hg->bhvgk", s, eye)
    return t.reshape(b, GLA_HEADS * GLA_DV, GLA_KW)


def _blockdiag_t_to_state(t):
    b = t.shape[0]
    t5 = t.reshape(b, GLA_HEADS, GLA_DV, GLA_HEADS, GLA_DK)
    d = jnp.stack([t5[:, h, :, h, :] for h in range(GLA_HEADS)], axis=1)
    return d.transpose(0, 1, 3, 2)


def _merge_kernel(x_ref, mod_ref, nw_ref, pc_ref, pprev_ref, pnext_ref, gc_ref, cw_ref,
                  yna_ref, ygla_ref, wg_ref, bg_ref, wb_ref, wo_ref, o_ref, *, tiles_per_seq):
    i = pl.program_id(0)
    x = x_ref[...]
    h = _modulated_norm(x, mod_ref, nw_ref).astype(BF16)
    tm = x.shape[0]

    pc = pc_ref[...].astype(F32)
    pos = i % tiles_per_seq
    prev_row = jnp.where(pos == 0, 0.0, pprev_ref[HALO - 1:HALO, :].astype(F32))
    next_row = jnp.where(pos == tiles_per_seq - 1, 0.0, pnext_ref[0:1, :].astype(F32))
    ridx = lax.broadcasted_iota(jnp.int32, (tm, 1), 0)
    before = jnp.where(ridx == 0, prev_row, pltpu.roll(pc, 1, axis=0))
    after = jnp.where(ridx == tm - 1, next_row, pltpu.roll(pc, tm - 1, axis=0))
    conv = before * cw_ref[0:1, :] + pc * cw_ref[1:2, :] + after * cw_ref[2:3, :]
    y_conv = (gc_ref[...].astype(F32) * conv).astype(BF16)

    merged = None
    for b, yb in enumerate((y_conv, yna_ref[...], ygla_ref[...])):
        cols = slice(b * D_MODEL, (b + 1) * D_MODEL)
        g = _sigmoid(_dot(h, wg_ref[:, cols]) + bg_ref[:, cols])
        t = g * _dot(yb, wb_ref[b])
        merged = t if merged is None else merged + t
    gate = mod_ref[0, :, 2 * D_MODEL:3 * D_MODEL]
    o_ref[...] = x + gate * _dot(merged.astype(BF16), wo_ref[...])


def _merge(x, mod, nw, pc, gc, cw, yna, ygla, wg, bg, wb, wo, seq):
    n_tok = x.shape[0]
    tm = min(TOKEN_TILE, seq)
    tiles_per_seq = seq // tm
    hb = tm // HALO
    last = n_tok // HALO - 1

    def tok(w):
        return pl.BlockSpec((tm, w), lambda i: (i, 0))

    def const(shape):
        return pl.BlockSpec(shape, lambda i: (0,) * len(shape))

    return pl.pallas_call(
        functools.partial(_merge_kernel, tiles_per_seq=tiles_per_seq),
        grid=(n_tok // tm,),
        in_specs=[
            tok(D_MODEL),
            pl.BlockSpec((1, 1, 3 * D_MODEL), lambda i: (i // tiles_per_seq, 0, 0)),
            const((1, D_MODEL)),
            tok(BRANCH_W),
            pl.BlockSpec((HALO, BRANCH_W), lambda i: (jnp.maximum(i * hb - 1, 0), 0)),
            pl.BlockSpec((HALO, BRANCH_W), lambda i: (jnp.minimum((i + 1) * hb, last), 0)),
            tok(BRANCH_W),
            const((3, BRANCH_W)),
            tok(BRANCH_W),
            tok(BRANCH_W),
            _resident((D_MODEL, N_BRANCH * D_MODEL)),
            const((1, N_BRANCH * D_MODEL)),
            _resident((N_BRANCH, BRANCH_W, D_MODEL)),
            _resident((D_MODEL, D_MODEL)),
        ],
        out_specs=tok(D_MODEL),
        out_shape=jax.ShapeDtypeStruct((n_tok, D_MODEL), F32),
        compiler_params=_cparams(1),
        name="merge",
    )(x, mod, nw, pc, pc, pc, gc, cw, yna, ygla, wg, bg, wb, wo)


def _rope_tables(seq):
    pos = np.arange(seq)
    n_f = GLA_DK // 4
    inv = ROPE_BASE ** (-np.arange(n_f) / n_f)
    ang_r = ((pos // GRID_W)[:, None] * inv).astype(np.float32).astype(np.float64)
    ang_c = ((pos % GRID_W)[:, None] * inv).astype(np.float32).astype(np.float64)
    zero = np.zeros_like(ang_r)
    cos = np.concatenate([np.cos(ang_r), np.cos(ang_r), np.cos(ang_c), np.cos(ang_c)], axis=1)
    s1 = np.concatenate([-np.sin(ang_r), zero, -np.sin(ang_c), zero], axis=1)
    s2 = np.concatenate([zero, np.sin(ang_r), zero, np.sin(ang_c)], axis=1)
    tile = lambda t: jnp.asarray(np.tile(t, (1, LANES // GLA_DK)).astype(np.float32))
    return tile(cos), tile(s1), tile(s2)


def _head_mean_matrix():
    h = np.arange(BRANCH_W) // NA_HD
    return jnp.asarray((h[:, None] == h[None, :]).astype(np.float32) / NA_HD, BF16)


def _gla_blockdiag_mask():
    r = np.arange(GLA_HEADS * GLA_DV) // GLA_DV
    c = np.arange(GLA_KW) // GLA_DK
    return jnp.asarray((r[:, None] == c[None, :]).astype(np.float32), F32)


def kernel(x_prompt, x_sample, c, cache_k, cache_v, state_fwd, state_bwd, c_ctx,
           norm_w, w_ada, b_ada, w_in, conv_w, q_norm_w, k_norm_w, rpb,
           w_alpha, b_alpha, gla_norm_w, w_branch, w_gate, b_gate, w_out):
    nb_p, seq_p, _ = x_prompt.shape
    nb_s, seq_s, _ = x_sample.shape
    past = cache_k.shape[3]

    cond = jnp.concatenate([c, c_ctx[None, :], jnp.zeros((16 - nb_s - 1, D_MODEL), F32)], axis=0)
    mods = _adaln(cond, w_ada, b_ada)

    w_in_b = jnp.pad(w_in, ((0, 0), (0, 0), (0, D_IN_PAD - D_IN))).astype(BF16)
    w_gate_b = w_gate.astype(BF16)
    w_branch_b = w_branch.astype(BF16)
    w_out_b = w_out.astype(BF16)
    hm = _head_mean_matrix()
    bd = _gla_blockdiag_mask()
    pm = _gla_sum_matrices()
    rope_tabs = _rope_tables(seq_s)

    ck = cache_k.transpose(0, 1, 3, 2, 4).reshape(nb_s, DEPTH, past, BRANCH_W).astype(BF16)
    cv = cache_v.transpose(0, 1, 3, 2, 4).reshape(nb_s, DEPTH, past, BRANCH_W).astype(BF16)

    y_p = x_prompt.reshape(nb_p * seq_p, D_MODEL)
    y_s = x_sample.reshape(nb_s * seq_s, D_MODEL)
    zeros_state = jnp.zeros((nb_p, GLA_HEADS * GLA_DV, GLA_KW), F32)
    ks, vs, sfs, sbs = [], [], [], []
    for l in range(DEPTH):
        nw = norm_w[l].reshape(1, D_MODEL)
        qw = jnp.tile(q_norm_w[l], NA_HEADS).reshape(1, BRANCH_W)
        kw = jnp.tile(k_norm_w[l], NA_HEADS).reshape(1, BRANCH_W)
        gw = jnp.tile(gla_norm_w[l], GLA_HEADS).reshape(1, BRANCH_W)
        wa = jnp.zeros((LANES, 2 * GLA_KW), F32)
        wa = wa.at[0:GLA_RANK, 0:GLA_KW].set(w_alpha[l, 0])
        wa = wa.at[GLA_RANK:2 * GLA_RANK, GLA_KW:].set(w_alpha[l, 1])
        ba = b_alpha[l].reshape(1, 2 * GLA_KW)
        bg = b_gate[l].reshape(1, N_BRANCH * D_MODEL)

        mod_p = mods[l, nb_s:nb_s + 1].reshape(1, 1, 3 * D_MODEL)
        mod_p = jnp.broadcast_to(mod_p, (nb_p, 1, 3 * D_MODEL))
        (pc, gc, q, k, v, gn, qg, kg, vg, gg, lr) = _inproj(
            y_p, mod_p, nw, w_in_b[l], hm, qw, kw, None, seq_p, F32)
        yna = _na_ctx(q, k, v, gn, nb_p, seq_p)
        ygla, sf, sb = _gla(qg, kg, vg, gg, lr, wa, ba, gw, bd, pm, zeros_state, zeros_state,
                            nb_p, seq_p)
        y_p = _merge(y_p, mod_p, nw, pc, gc, conv_w[l], yna, ygla,
                     w_gate_b[l], bg, w_branch_b[l], w_out_b[l], seq_p)
        ks.append(k.reshape(nb_p, seq_p, NA_HEADS, NA_HD).transpose(0, 2, 1, 3))
        vs.append(v.reshape(nb_p, seq_p, NA_HEADS, NA_HD).transpose(0, 2, 1, 3))
        sfs.append(_blockdiag_t_to_state(sf))
        sbs.append(_blockdiag_t_to_state(sb))

        mod_s = mods[l, 0:nb_s].reshape(nb_s, 1, 3 * D_MODEL)
        (pc, gc, q, k, v, gn, qg, kg, vg, gg, lr) = _inproj(
            y_s, mod_s, nw, w_in_b[l], hm, qw, kw, rope_tabs, seq_s, BF16)
        bias = _na_bias_table(rpb[l], seq_s // GRID_W)
        yna = _na_lat(q, k, v, ck[:, l], cv[:, l], bias, gn, nb_s, seq_s)
        ygla, _, _ = _gla(qg, kg, vg, gg, lr, wa, ba, gw, bd, pm,
                          _state_to_blockdiag_t(state_fwd[:, l]),
                          _state_to_blockdiag_t(state_bwd[:, l]), nb_s, seq_s)
        y_s = _merge(y_s, mod_s, nw, pc, gc, conv_w[l], yna, ygla,
                     w_gate_b[l], bg, w_branch_b[l], w_out_b[l], seq_s)

    return (y_p.reshape(nb_p, seq_p, D_MODEL), y_s.reshape(nb_s, seq_s, D_MODEL),
            jnp.stack(ks, axis=1), jnp.stack(vs, axis=1),
            jnp.stack(sfs, axis=1), jnp.stack(sbs, axis=1))
```

```python
import functools

import numpy as np
import jax
import jax.numpy as jnp
from jax import lax
from jax.experimental import pallas as pl
from jax.experimental.pallas import tpu as pltpu

F32 = jnp.float32
BF16 = jnp.bfloat16

D_MODEL = 1024
DEPTH = 4
GRID_W = 64
EPS = 1e-6
NEG_INF = -1e30
BRANCH_W = D_MODEL // 2
N_BRANCH = 3
NA_HEADS = 8
NA_HD = BRANCH_W // NA_HEADS
WIN_R = 8
WIN_C = 16
GLA_HEADS = 4
GLA_DV = BRANCH_W // GLA_HEADS
GLA_DK = GLA_DV // 2
GLA_KW = GLA_HEADS * GLA_DK
GLA_RANK = 16
GLA_TAU = 16.0
ROPE_BASE = 10000.0
LOG2E = 1.4426950408889634

LANES = 128
D_IN = 8 * BRANCH_W + 2 * GLA_KW + 2 * BRANCH_W + 2 * GLA_RANK
D_IN_PAD = ((D_IN + LANES - 1) // LANES) * LANES
C_XA, C_BA, C_CA, C_GA = 0, 512, 1024, 1536
C_QN, C_KN, C_VN, C_GN = 2048, 2560, 3072, 3584
C_QG, C_KG, C_VG, C_GG, C_LR = 4096, 4352, 4608, 5120, 5632

VMEM_LIMIT = 56 * 1024 * 1024
TOKEN_TILE = 1024
GLA_CHUNK = 64
GLA_TILE = 256
NA_QROWS = 4
NA_KROWS = 12
HALO = 16


def _cparams(n_axes):
    return pltpu.CompilerParams(dimension_semantics=("arbitrary",) * n_axes,
                                vmem_limit_bytes=VMEM_LIMIT)


def _resident(shape):
    return pl.BlockSpec(shape, lambda i: (0,) * len(shape), pipeline_mode=pl.Buffered(1))


def _sigmoid(x):
    return 1.0 / (1.0 + jnp.exp(-x))


def _silu(x):
    return x * _sigmoid(x)


def _log_sigmoid(x):
    return jnp.minimum(x, 0.0) - jnp.log(1.0 + jnp.exp(-jnp.abs(x)))


def _dot(a, b):
    return jnp.dot(a, b, preferred_element_type=F32)


def _dot_nt(a, b):
    return lax.dot_general(a, b, (((1,), (1,)), ((), ())), preferred_element_type=F32)


def _dot_tn(a, b):
    return lax.dot_general(a, b, (((0,), (0,)), ((), ())), preferred_element_type=F32)


def _ada_kernel(cond_ref, w_ref, b_ref, o_ref):
    c = cond_ref[...]
    o_ref[0] = _dot(_silu(c).astype(BF16), w_ref[0].astype(BF16)) + b_ref[0]


def _adaln(cond, w_ada, b_ada):
    r = cond.shape[0]
    tn = 768
    return pl.pallas_call(
        _ada_kernel,
        grid=(DEPTH, 3 * D_MODEL // tn),
        in_specs=[
            pl.BlockSpec((r, D_MODEL), lambda l, n: (0, 0)),
            pl.BlockSpec((1, D_MODEL, tn), lambda l, n: (l, 0, n)),
            pl.BlockSpec((1, 1, tn), lambda l, n: (l, 0, n)),
        ],
        out_specs=pl.BlockSpec((1, r, tn), lambda l, n: (l, 0, n)),
        out_shape=jax.ShapeDtypeStruct((DEPTH, r, 3 * D_MODEL), F32),
        compiler_params=_cparams(2),
        name="adaln",
    )(cond, w_ada, b_ada.reshape(DEPTH, 1, 3 * D_MODEL))


def _modulated_norm(x, mod_ref, nw_ref):
    ms = jnp.mean(x * x, axis=-1, keepdims=True)
    xn = x * lax.rsqrt(ms + EPS) * nw_ref[...]
    shift = mod_ref[0, :, 0:D_MODEL]
    scale = mod_ref[0, :, D_MODEL:2 * D_MODEL]
    return xn * (1.0 + scale) + shift


def _inproj_kernel(*refs, rope):
    if rope:
        (x_ref, mod_ref, nw_ref, w_ref, hm_ref, qw_ref, kw_ref, rc_ref, rs1_ref, rs2_ref,
         pc_ref, gc_ref, q_ref, k_ref, v_ref, gn_ref, qg_ref, kg_ref, vg_ref, gg_ref, lr_ref) = refs
    else:
        (x_ref, mod_ref, nw_ref, w_ref, hm_ref, qw_ref, kw_ref,
         pc_ref, gc_ref, q_ref, k_ref, v_ref, gn_ref, qg_ref, kg_ref, vg_ref, gg_ref, lr_ref) = refs
    h = _modulated_norm(x_ref[...], mod_ref, nw_ref).astype(BF16)

    def proj(c0, n):
        return _dot(h, w_ref[:, c0:c0 + n])

    pc_ref[...] = (proj(C_CA, BRANCH_W) * proj(C_XA, BRANCH_W)).astype(pc_ref.dtype)
    gc_ref[...] = (proj(C_BA, BRANCH_W) * _silu(proj(C_GA, BRANCH_W))).astype(gc_ref.dtype)

    def head_norm(u, g_ref):
        ms = _dot((u * u).astype(BF16), hm_ref[...])
        return u * lax.rsqrt(ms + EPS) * g_ref[...]

    q_ref[...] = (head_norm(proj(C_QN, BRANCH_W), qw_ref) * (NA_HD ** -0.5 * LOG2E)).astype(q_ref.dtype)
    k_ref[...] = head_norm(proj(C_KN, BRANCH_W), kw_ref).astype(k_ref.dtype)
    v_ref[...] = proj(C_VN, BRANCH_W).astype(v_ref.dtype)
    gn_ref[...] = _silu(proj(C_GN, BRANCH_W)).astype(gn_ref.dtype)

    qg = proj(C_QG, GLA_KW) * (GLA_DK ** -0.5)
    kg = proj(C_KG, GLA_KW)
    if rope:
        def rot(u):
            parts = []
            for j in range(GLA_KW // LANES):
                uj = u[:, j * LANES:(j + 1) * LANES]
                parts.append(uj * rc_ref[...]
                             + pltpu.roll(uj, LANES - GLA_DK // 4, axis=1) * rs1_ref[...]
                             + pltpu.roll(uj, GLA_DK // 4, axis=1) * rs2_ref[...])
            return jnp.concatenate(parts, axis=1)
        qg = rot(qg)
        kg = rot(kg)
    qg_ref[...] = qg.astype(qg_ref.dtype)
    kg_ref[...] = kg.astype(kg_ref.dtype)
    vg_ref[...] = proj(C_VG, BRANCH_W).astype(vg_ref.dtype)
    gg_ref[...] = _silu(proj(C_GG, BRANCH_W)).astype(gg_ref.dtype)
    lr_ref[...] = proj(C_LR, LANES)


def _inproj(x, mod, nw, w_in, hm, qw, kw, rope_tabs, seq, kv_dtype):
    n_tok = x.shape[0]
    tm = min(TOKEN_TILE, seq)
    tiles_per_seq = seq // tm
    rope = rope_tabs is not None

    def tok(w):
        return pl.BlockSpec((tm, w), lambda i: (i, 0))

    def const(shape):
        return pl.BlockSpec(shape, lambda i: (0,) * len(shape))

    in_specs = [
        tok(D_MODEL),
        pl.BlockSpec((1, 1, 3 * D_MODEL), lambda i: (i // tiles_per_seq, 0, 0)),
        const((1, D_MODEL)),
        _resident((D_MODEL, D_IN_PAD)),
        const((BRANCH_W, BRANCH_W)),
        const((1, BRANCH_W)),
        const((1, BRANCH_W)),
    ]
    args = [x, mod, nw, w_in, hm, qw, kw]
    if rope:
        in_specs += [pl.BlockSpec((tm, LANES), lambda i: (i % tiles_per_seq, 0))] * 3
        args += list(rope_tabs)
    widths = [BRANCH_W, BRANCH_W, BRANCH_W, BRANCH_W, BRANCH_W, BRANCH_W,
              GLA_KW, GLA_KW, BRANCH_W, BRANCH_W, LANES]
    dtypes = [BF16, BF16, BF16, kv_dtype, kv_dtype, BF16, BF16, BF16, BF16, BF16, F32]
    return pl.pallas_call(
        functools.partial(_inproj_kernel, rope=rope),
        grid=(n_tok // tm,),
        in_specs=in_specs,
        out_specs=[tok(w) for w in widths],
        out_shape=[jax.ShapeDtypeStruct((n_tok, w), d) for w, d in zip(widths, dtypes)],
        compiler_params=_cparams(1),
        name="inproj_rope" if rope else "inproj",
    )(*args)


def _pair_masks(dtype):
    lane = lax.broadcasted_iota(jnp.int32, (1, LANES), 1)
    return lane < NA_HD, lane >= NA_HD


def _softmax_pv(scores, values, head_lanes):
    m = functools.reduce(jnp.maximum, [jnp.max(s, axis=-1, keepdims=True) for s in scores])
    o = None
    for s, v in zip(scores, values):
        p = jnp.exp2(s - m).astype(BF16)
        t = _dot(p, jnp.where(head_lanes, v, jnp.ones_like(v)))
        o = t if o is None else o + t
    return o / pltpu.roll(o, NA_HD, axis=1)


def _na_ctx_kernel(q_ref, k_ref, v_ref, gn_ref, o_ref):
    m0, m1 = _pair_masks(BF16)
    for hp in range(NA_HEADS // 2):
        cols = slice(hp * LANES, (hp + 1) * LANES)
        qp = q_ref[0, :, cols]
        kp = k_ref[0, :, cols].astype(BF16)
        vp = v_ref[0, :, cols].astype(BF16)
        outs = []
        for msk in (m0, m1):
            qa = jnp.where(msk, qp, jnp.zeros_like(qp))
            outs.append(_softmax_pv([_dot_nt(qa, kp)], [vp], msk))
        o = jnp.where(m0, outs[0], outs[1])
        o_ref[0, :, cols] = (o * gn_ref[0, :, cols].astype(F32)).astype(o_ref.dtype)


def _na_ctx(q, k, v, gn, batch, seq):
    def blk():
        return pl.BlockSpec((1, seq, BRANCH_W), lambda b: (b, 0, 0))
    r3 = lambda u: u.reshape(batch, seq, BRANCH_W)
    out = pl.pallas_call(
        _na_ctx_kernel,
        grid=(batch,),
        in_specs=[blk(), blk(), blk(), blk()],
        out_specs=blk(),
        out_shape=jax.ShapeDtypeStruct((batch, seq, BRANCH_W), BF16),
        compiler_params=_cparams(1),
        name="na_ctx",
    )(r3(q), r3(k), r3(v), r3(gn))
    return out.reshape(batch * seq, BRANCH_W)


def _na_window_start(j, rows):
    return jnp.clip(j * NA_QROWS - WIN_R // 2, 0, rows - NA_KROWS)


def _na_lat_kernel(q_ref, k_ref, v_ref, kc_ref, vc_ref, bias_ref, gn_ref, o_ref, *, rows):
    j = pl.program_id(1)
    start = pl.multiple_of(_na_window_start(j, rows) * GRID_W, NA_QROWS * GRID_W)
    m0, m1 = _pair_masks(BF16)
    for hp in range(NA_HEADS // 2):
        cols = slice(hp * LANES, (hp + 1) * LANES)
        qp = q_ref[0, :, cols]
        kw = k_ref[0, pl.ds(start, NA_KROWS * GRID_W), cols]
        vw = v_ref[0, pl.ds(start, NA_KROWS * GRID_W), cols]
        kc = kc_ref[0, :, cols]
        vc = vc_ref[0, :, cols]
        tq = qp.shape[0]
        q2 = jnp.concatenate([jnp.where(m0, qp, jnp.zeros_like(qp)),
                              jnp.where(m1, qp, jnp.zeros_like(qp))], axis=0)
        bias = bias_ref[0, 2 * hp:2 * hp + 2].reshape(2 * tq, NA_KROWS * GRID_W)
        s_w = _dot_nt(q2, kw) + bias.astype(F32)
        s_c = _dot_nt(q2, kc)
        outs = []
        for a, msk in enumerate((m0, m1)):
            rows_a = slice(a * tq, (a + 1) * tq)
            outs.append(_softmax_pv([s_w[rows_a], s_c[rows_a]], [vw, vc], msk))
        o = jnp.where(m0, outs[0], outs[1])
        o_ref[0, :, cols] = (o * gn_ref[0, :, cols].astype(F32)).astype(o_ref.dtype)


def _na_lat(q, k, v, kc, vc, bias, gn, batch, seq):
    rows = seq // GRID_W
    nj = rows // NA_QROWS
    tq = NA_QROWS * GRID_W

    def bias_type(j):
        return jnp.where(j == 0, 0, jnp.where(j == nj - 1, 2, 1))

    qblk = pl.BlockSpec((1, tq, BRANCH_W), lambda b, j: (b, j, 0))
    seqblk = pl.BlockSpec((1, seq, BRANCH_W), lambda b, j: (b, 0, 0))
    ctxblk = pl.BlockSpec((1, kc.shape[1], BRANCH_W), lambda b, j: (b, 0, 0))
    r3 = lambda u: u.reshape(batch, seq, BRANCH_W)
    out = pl.pallas_call(
        functools.partial(_na_lat_kernel, rows=rows),
        grid=(batch, nj),
        in_specs=[qblk, seqblk, seqblk, ctxblk, ctxblk,
                  pl.BlockSpec((1, NA_HEADS, tq, NA_KROWS * GRID_W),
                               lambda b, j: (bias_type(j), 0, 0, 0)),
                  qblk],
        out_specs=qblk,
        out_shape=jax.ShapeDtypeStruct((batch, seq, BRANCH_W), BF16),
        compiler_params=_cparams(2),
        name="na_lat",
    )(r3(q), r3(k), r3(v), kc, vc, bias, r3(gn))
    return out.reshape(batch * seq, BRANCH_W)


def _na_bias_kernel(rpb_ref, o_ref, *, rows):
    nj = rows // NA_QROWS
    qc = lax.broadcasted_iota(jnp.int32, (GRID_W, LANES), 0)
    lane = lax.broadcasted_iota(jnp.int32, (GRID_W, LANES), 1)
    kc = lane & (GRID_W - 1)
    cs = jnp.clip(qc - WIN_C // 2, 0, GRID_W - WIN_C)
    col_ok = (kc >= cs) & (kc < cs + WIN_C)
    left = lane < GRID_W
    neg = jnp.full((GRID_W, LANES), NEG_INF, F32)
    tl, tr = [], []
    for d in range(2 * WIN_R - 1):
        x = jnp.broadcast_to(rpb_ref[0, d:d + 1, :] * LOG2E, (GRID_W, LANES))
        tl.append(pltpu.roll(x, LANES - (WIN_C - 1), 1, stride=1, stride_axis=0))
        tr.append(pltpu.roll(x, GRID_W - (WIN_C - 1), 1, stride=1, stride_axis=0))
    for t, j in enumerate((0, 1, nj - 1)):
        start = int(np.clip(j * NA_QROWS - WIN_R // 2, 0, rows - NA_KROWS))
        for rl in range(NA_QROWS):
            r = j * NA_QROWS + rl
            rs = int(np.clip(r - WIN_R // 2, 0, rows - WIN_R))
            for kp in range(NA_KROWS // 2):
                k0 = start + 2 * kp
                a = tl[k0 - r + WIN_R - 1] if rs <= k0 < rs + WIN_R else neg
                b = tr[k0 + 1 - r + WIN_R - 1] if rs <= k0 + 1 < rs + WIN_R else neg
                blk = jnp.where(col_ok, jnp.where(left, a, b), NEG_INF)
                o_ref[t, 0, rl * GRID_W:(rl + 1) * GRID_W, kp * LANES:(kp + 1) * LANES] = (
                    blk.astype(o_ref.dtype))


def _na_bias_table(rpb, rows):
    rp = jnp.pad(rpb, ((0, 0), (0, 0), (0, LANES - (2 * WIN_C - 1))))
    tq, tk = NA_QROWS * GRID_W, NA_KROWS * GRID_W
    return pl.pallas_call(
        functools.partial(_na_bias_kernel, rows=rows),
        grid=(NA_HEADS,),
        in_specs=[pl.BlockSpec((1, 2 * WIN_R - 1, LANES), lambda h: (h, 0, 0))],
        out_specs=pl.BlockSpec((3, 1, tq, tk), lambda h: (0, h, 0, 0)),
        out_shape=jax.ShapeDtypeStruct((3, NA_HEADS, tq, tk), BF16),
        compiler_params=_cparams(1),
        name="na_bias",
    )(rp)


def _gla_kernel(q_ref, k_ref, v_ref, gg_ref, lr_ref, wa_ref, ba_ref, gw_ref, bd_ref, pm_ref,
                s0f_ref, s0b_ref, y_ref, sf_ref, sb_ref,
                of_ref, ob_ref, stf_ref, stb_ref, qi_ref, ku_ref, dec_ref, *, seq):
    c = GLA_CHUNK
    n = seq // c
    t = GLA_TILE
    lane = lax.broadcasted_iota(jnp.int32, (1, GLA_KW), 1)
    head_masks = [(lane >= h * GLA_DK) & (lane < (h + 1) * GLA_DK) for h in range(GLA_HEADS)]
    row = lax.broadcasted_iota(jnp.int32, (t, t), 0)
    col = lax.broadcasted_iota(jnp.int32, (t, t), 1)
    same_chunk = (row // c) == (col // c)
    causal = (same_chunk & (row >= col), same_chunk & (row <= col))
    wa = wa_ref[...].astype(BF16)
    outs = (of_ref, ob_ref)

    def chunk_rows(x, r):
        return jnp.concatenate(
            [jnp.broadcast_to(x[j * c + r:j * c + r + 1], (c, GLA_KW)) for j in range(t // c)], axis=0)

    def prep(i, carry):
        sl = pl.ds(pl.multiple_of(i * t, t), t)
        lr = lr_ref[0, sl, :].astype(BF16)
        q = q_ref[0, sl, :].astype(F32)
        k = k_ref[0, sl, :].astype(F32)
        v = v_ref[0, sl, :]
        for d in range(2):
            cols = slice(d * GLA_KW, (d + 1) * GLA_KW)
            la = _log_sigmoid(_dot(lr, wa[:, cols]) + ba_ref[:, cols]) * (1.0 / GLA_TAU)
            la_hi = la.astype(BF16)
            la_lo = (la - la_hi.astype(F32)).astype(BF16)
            g_cum = _dot(pm_ref[d], la_hi) + _dot(pm_ref[d], la_lo)
            g_all = chunk_rows(g_cum, (c - 1, 0)[d])
            g_ref = g_cum - chunk_rows(g_cum, (c // 2 - 1, c // 2)[d])
            qd = (q * jnp.exp(g_ref)).astype(BF16)
            kd = (k * jnp.exp(-g_ref)).astype(BF16)
            qi_ref[d, sl, :] = (q * jnp.exp(g_cum)).astype(BF16)
            ku_ref[d, sl, :] = (k * jnp.exp(g_all - g_cum)).astype(BF16)
            dec_ref[d, sl, :] = jnp.exp(g_all)
            parts = []
            for h in range(GLA_HEADS):
                qh = jnp.where(head_masks[h], qd, jnp.zeros_like(qd))
                att = jnp.where(causal[d], _dot_nt(qh, kd), 0.0)
                parts.append(_dot(att.astype(BF16), v[:, h * GLA_DV:(h + 1) * GLA_DV]))
            outs[d][sl, :] = jnp.concatenate(parts, axis=1)
        return carry

    lax.fori_loop(0, seq // t, prep, 0, unroll=2)

    stf_ref[...] = s0f_ref[0]
    stb_ref[...] = s0b_ref[0]

    def step(d, r0, st_ref):
        sl = pl.ds(r0, c)
        st = st_ref[...]
        outs[d][sl, :] += _dot_nt(qi_ref[d, sl, :], st.astype(BF16))
        upd = _dot_tn(v_ref[0, sl, :], ku_ref[d, sl, :])
        st_ref[...] = st * dec_ref[d, pl.ds(r0, 1), :] + upd * bd_ref[...]

    def scan(i, carry):
        step(0, pl.multiple_of(i * c, c), stf_ref)
        step(1, pl.multiple_of((n - 1 - i) * c, c), stb_ref)
        return carry

    lax.fori_loop(0, n, scan, 0, unroll=4)
    sf_ref[0] = stf_ref[...]
    sb_ref[0] = stb_ref[...]

    def finish(i, carry):
        sl = pl.ds(pl.multiple_of(i * t, t), t)
        o = of_ref[sl, :] + ob_ref[sl, :]
        parts = []
        for h in range(GLA_HEADS):
            oh = o[:, h * GLA_DV:(h + 1) * GLA_DV]
            ms = jnp.mean(oh * oh, axis=-1, keepdims=True)
            parts.append(oh * lax.rsqrt(ms + EPS))
        y = jnp.concatenate(parts, axis=1) * gw_ref[...]
        y_ref[0, sl, :] = (y * gg_ref[0, sl, :].astype(F32)).astype(y_ref.dtype)
        return carry

    lax.fori_loop(0, seq // t, finish, 0)


def _gla_sum_matrices():
    t, c = GLA_TILE, GLA_CHUNK
    r = np.arange(t)[:, None]
    j = np.arange(t)[None, :]
    same = (r // c) == (j // c)
    mats = [(same & (j <= r)).astype(np.float32), (same & (j >= r)).astype(np.float32)]
    return jnp.asarray(np.stack(mats), BF16)


def _gla(qg, kg, vg, gg, lr, wa, ba, gw, bd, pm, s0f, s0b, batch, seq):
    hv, hk = GLA_HEADS * GLA_DV, GLA_KW
    assert seq % GLA_TILE == 0

    def seqblk(w):
        return pl.BlockSpec((1, seq, w), lambda b: (b, 0, 0))

    def const(shape):
        return pl.BlockSpec(shape, lambda b: (0,) * len(shape))

    stblk = pl.BlockSpec((1, hv, hk), lambda b: (b, 0, 0))
    r3 = lambda u: u.reshape(batch, seq, u.shape[-1])
    y, sf, sb = pl.pallas_call(
        functools.partial(_gla_kernel, seq=seq),
        grid=(batch,),
        in_specs=[seqblk(hk), seqblk(hk), seqblk(hv), seqblk(hv), seqblk(LANES),
                  const((LANES, 2 * hk)), const((1, 2 * hk)), const((1, hv)), const((hv, hk)),
                  const((2, GLA_TILE, GLA_TILE)), stblk, stblk],
        out_specs=[seqblk(hv), stblk, stblk],
        out_shape=[jax.ShapeDtypeStruct((batch, seq, hv), BF16),
                   jax.ShapeDtypeStruct((batch, hv, hk), F32),
                   jax.ShapeDtypeStruct((batch, hv, hk), F32)],
        scratch_shapes=[pltpu.VMEM((seq, hv), F32), pltpu.VMEM((seq, hv), F32),
                        pltpu.VMEM((hv, hk), F32), pltpu.VMEM((hv, hk), F32),
                        pltpu.VMEM((2, seq, hk), BF16), pltpu.VMEM((2, seq, hk), BF16),
                        pltpu.VMEM((2, seq, hk), F32)],
        compiler_params=_cparams(1),
        name="gla",
    )(r3(qg), r3(kg), r3(vg), r3(gg), r3(lr), wa, ba, gw, bd, pm, s0f, s0b)
    return y.reshape(batch * seq, hv), sf, sb


def _state_to_blockdiag_t(s):
    b = s.shape[0]
    eye = jnp.eye(GLA_HEADS, dtype=s.dtype)
    t = jnp.einsum("bhkv,hg->bhvgk", s, eye)
    return t.reshape(b, GLA_HEADS * GLA_DV, GLA_KW)


def _blockdiag_t_to_state(t):
    b = t.shape[0]
    t5 = t.reshape(b, GLA_HEADS, GLA_DV, GLA_HEADS, GLA_DK)
    d = jnp.stack([t5[:, h, :, h, :] for h in range(GLA_HEADS)], axis=1)
    return d.transpose(0, 1, 3, 2)


def _merge_kernel(x_ref, mod_ref, nw_ref, pc_ref, pprev_ref, pnext_ref, gc_ref, cw_ref,
                  yna_ref, ygla_ref, wg_ref, bg_ref, wb_ref, wo_ref, o_ref, *, tiles_per_seq):
    i = pl.program_id(0)
    x = x_ref[...]
    h = _modulated_norm(x, mod_ref, nw_ref).astype(BF16)
    tm = x.shape[0]

    pc = pc_ref[...].astype(F32)
    pos = i % tiles_per_seq
    prev_row = jnp.where(pos == 0, 0.0, pprev_ref[HALO - 1:HALO, :].astype(F32))
    next_row = jnp.where(pos == tiles_per_seq - 1, 0.0, pnext_ref[0:1, :].astype(F32))
    ridx = lax.broadcasted_iota(jnp.int32, (tm, 1), 0)
    before = jnp.where(ridx == 0, prev_row, pltpu.roll(pc, 1, axis=0))
    after = jnp.where(ridx == tm - 1, next_row, pltpu.roll(pc, tm - 1, axis=0))
    conv = before * cw_ref[0:1, :] + pc * cw_ref[1:2, :] + after * cw_ref[2:3, :]
    y_conv = (gc_ref[...].astype(F32) * conv).astype(BF16)

    merged = None
    for b, yb in enumerate((y_conv, yna_ref[...], ygla_ref[...])):
        cols = slice(b * D_MODEL, (b + 1) * D_MODEL)
        g = _sigmoid(_dot(h, wg_ref[:, cols]) + bg_ref[:, cols])
        t = g * _dot(yb, wb_ref[b])
        merged = t if merged is None else merged + t
    gate = mod_ref[0, :, 2 * D_MODEL:3 * D_MODEL]
    o_ref[...] = x + gate * _dot(merged.astype(BF16), wo_ref[...])


def _merge(x, mod, nw, pc, gc, cw, yna, ygla, wg, bg, wb, wo, seq):
    n_tok = x.shape[0]
    tm = min(TOKEN_TILE, seq)
    tiles_per_seq = seq // tm
    hb = tm // HALO
    last = n_tok // HALO - 1

    def tok(w):
        return pl.BlockSpec((tm, w), lambda i: (i, 0))

    def const(shape):
        return pl.BlockSpec(shape, lambda i: (0,) * len(shape))

    return pl.pallas_call(
        functools.partial(_merge_kernel, tiles_per_seq=tiles_per_seq),
        grid=(n_tok // tm,),
        in_specs=[
            tok(D_MODEL),
            pl.BlockSpec((1, 1, 3 * D_MODEL), lambda i: (i // tiles_per_seq, 0, 0)),
            const((1, D_MODEL)),
            tok(BRANCH_W),
            pl.BlockSpec((HALO, BRANCH_W), lambda i: (jnp.maximum(i * hb - 1, 0), 0)),
            pl.BlockSpec((HALO, BRANCH_W), lambda i: (jnp.minimum((i + 1) * hb, last), 0)),
            tok(BRANCH_W),
            const((3, BRANCH_W)),
            tok(BRANCH_W),
            tok(BRANCH_W),
            _resident((D_MODEL, N_BRANCH * D_MODEL)),
            const((1, N_BRANCH * D_MODEL)),
            _resident((N_BRANCH, BRANCH_W, D_MODEL)),
            _resident((D_MODEL, D_MODEL)),
        ],
        out_specs=tok(D_MODEL),
        out_shape=jax.ShapeDtypeStruct((n_tok, D_MODEL), F32),
        compiler_params=_cparams(1),
        name="merge",
    )(x, mod, nw, pc, pc, pc, gc, cw, yna, ygla, wg, bg, wb, wo)


def _rope_tables(seq):
    pos = np.arange(seq)
    n_f = GLA_DK // 4
    inv = ROPE_BASE ** (-np.arange(n_f) / n_f)
    ang_r = ((pos // GRID_W)[:, None] * inv).astype(np.float32).astype(np.float64)
    ang_c = ((pos % GRID_W)[:, None] * inv).astype(np.float32).astype(np.float64)
    zero = np.zeros_like(ang_r)
    cos = np.concatenate([np.cos(ang_r), np.cos(ang_r), np.cos(ang_c), np.cos(ang_c)], axis=1)
    s1 = np.concatenate([-np.sin(ang_r), zero, -np.sin(ang_c), zero], axis=1)
    s2 = np.concatenate([zero, np.sin(ang_r), zero, np.sin(ang_c)], axis=1)
    tile = lambda t: jnp.asarray(np.tile(t, (1, LANES // GLA_DK)).astype(np.float32))
    return tile(cos), tile(s1), tile(s2)


def _head_mean_matrix():
    h = np.arange(BRANCH_W) // NA_HD
    return jnp.asarray((h[:, None] == h[None, :]).astype(np.float32) / NA_HD, BF16)


def _gla_blockdiag_mask():
    r = np.arange(GLA_HEADS * GLA_DV) // GLA_DV
    c = np.arange(GLA_KW) // GLA_DK
    return jnp.asarray((r[:, None] == c[None, :]).astype(np.float32), F32)


def kernel(x_prompt, x_sample, c, cache_k, cache_v, state_fwd, state_bwd, c_ctx,
           norm_w, w_ada, b_ada, w_in, conv_w, q_norm_w, k_norm_w, rpb,
           w_alpha, b_alpha, gla_norm_w, w_branch, w_gate, b_gate, w_out):
    nb_p, seq_p, _ = x_prompt.shape
    nb_s, seq_s, _ = x_sample.shape
    past = cache_k.shape[3]

    cond = jnp.concatenate([c, c_ctx[None, :], jnp.zeros((16 - nb_s - 1, D_MODEL), F32)], axis=0)
    mods = _adaln(cond, w_ada, b_ada)

    w_in_b = jnp.pad(w_in, ((0, 0), (0, 0), (0, D_IN_PAD - D_IN))).astype(BF16)
    w_gate_b = w_gate.astype(BF16)
    w_branch_b = w_branch.astype(BF16)
    w_out_b = w_out.astype(BF16)
    hm = _head_mean_matrix()
    bd = _gla_blockdiag_mask()
    pm = _gla_sum_matrices()
    rope_tabs = _rope_tables(seq_s)

    ck = cache_k.transpose(0, 1, 3, 2, 4).reshape(nb_s, DEPTH, past, BRANCH_W).astype(BF16)
    cv = cache_v.transpose(0, 1, 3, 2, 4).reshape(nb_s, DEPTH, past, BRANCH_W).astype(BF16)

    y_p = x_prompt.reshape(nb_p * seq_p, D_MODEL)
    y_s = x_sample.reshape(nb_s * seq_s, D_MODEL)
    zeros_state = jnp.zeros((nb_p, GLA_HEADS * GLA_DV, GLA_KW), F32)
    ks, vs, sfs, sbs = [], [], [], []
    for l in range(DEPTH):
        nw = norm_w[l].reshape(1, D_MODEL)
        qw = jnp.tile(q_norm_w[l], NA_HEADS).reshape(1, BRANCH_W)
        kw = jnp.tile(k_norm_w[l], NA_HEADS).reshape(1, BRANCH_W)
        gw = jnp.tile(gla_norm_w[l], GLA_HEADS).reshape(1, BRANCH_W)
        wa = jnp.zeros((LANES, 2 * GLA_KW), F32)
        wa = wa.at[0:GLA_RANK, 0:GLA_KW].set(w_alpha[l, 0])
        wa = wa.at[GLA_RANK:2 * GLA_RANK, GLA_KW:].set(w_alpha[l, 1])
        ba = b_alpha[l].reshape(1, 2 * GLA_KW)
        bg = b_gate[l].reshape(1, N_BRANCH * D_MODEL)

        mod_p = mods[l, nb_s:nb_s + 1].reshape(1, 1, 3 * D_MODEL)
        mod_p = jnp.broadcast_to(mod_p, (nb_p, 1, 3 * D_MODEL))
        (pc, gc, q, k, v, gn, qg, kg, vg, gg, lr) = _inproj(
            y_p, mod_p, nw, w_in_b[l], hm, qw, kw, None, seq_p, F32)
        yna = _na_ctx(q, k, v, gn, nb_p, seq_p)
        ygla, sf, sb = _gla(qg, kg, vg, gg, lr, wa, ba, gw, bd, pm, zeros_state, zeros_state,
                            nb_p, seq_p)
        y_p = _merge(y_p, mod_p, nw, pc, gc, conv_w[l], yna, ygla,
                     w_gate_b[l], bg, w_branch_b[l], w_out_b[l], seq_p)
        ks.append(k.reshape(nb_p, seq_p, NA_HEADS, NA_HD).transpose(0, 2, 1, 3))
        vs.append(v.reshape(nb_p, seq_p, NA_HEADS, NA_HD).transpose(0, 2, 1, 3))
        sfs.append(_blockdiag_t_to_state(sf))
        sbs.append(_blockdiag_t_to_state(sb))

        mod_s = mods[l, 0:nb_s].reshape(nb_s, 1, 3 * D_MODEL)
        (pc, gc, q, k, v, gn, qg, kg, vg, gg, lr) = _inproj(
            y_s, mod_s, nw, w_in_b[l], hm, qw, kw, rope_tabs, seq_s, BF16)
        bias = _na_bias_table(rpb[l], seq_s // GRID_W)
        yna = _na_lat(q, k, v, ck[:, l], cv[:, l], bias, gn, nb_s, seq_s)
        ygla, _, _ = _gla(qg, kg, vg, gg, lr, wa, ba, gw, bd, pm,
                          _state_to_blockdiag_t(state_fwd[:, l]),
                          _state_to_blockdiag_t(state_bwd[:, l]), nb_s, seq_s)
        y_s = _merge(y_s, mod_s, nw, pc, gc, conv_w[l], yna, ygla,
                     w_gate_b[l], bg, w_branch_b[l], w_out_b[l], seq_s)

    return (y_p.reshape(nb_p, seq_p, D_MODEL), y_s.reshape(nb_s, seq_s, D_MODEL),
            jnp.stack(ks, axis=1), jnp.stack(vs, axis=1),
            jnp.stack(sfs, axis=1), jnp.stack(sbs, axis=1))
```

```python
import functools

import numpy as np
import jax
import jax.numpy as jnp
from jax import lax
from jax.experimental import pallas as pl
from jax.experimental.pallas import tpu as pltpu

F32 = jnp.float32
BF16 = jnp.bfloat16

D_MODEL = 1024
DEPTH = 4
GRID_W = 64
EPS = 1e-6
NEG_INF = -1e30
BRANCH_W = D_MODEL // 2
N_BRANCH = 3
NA_HEADS = 8
NA_HD = BRANCH_W // NA_HEADS
WIN_R = 8
WIN_C = 16
GLA_HEADS = 4
GLA_DV = BRANCH_W // GLA_HEADS
GLA_DK = GLA_DV // 2
GLA_KW = GLA_HEADS * GLA_DK
GLA_RANK = 16
GLA_TAU = 16.0
ROPE_BASE = 10000.0
LOG2E = 1.4426950408889634

LANES = 128
D_IN = 8 * BRANCH_W + 2 * GLA_KW + 2 * BRANCH_W + 2 * GLA_RANK
D_IN_PAD = ((D_IN + LANES - 1) // LANES) * LANES
C_XA, C_BA, C_CA, C_GA = 0, 512, 1024, 1536
C_QN, C_KN, C_VN, C_GN = 2048, 2560, 3072, 3584
C_QG, C_KG, C_VG, C_GG, C_LR = 4096, 4352, 4608, 5120, 5632

VMEM_LIMIT = 56 * 1024 * 1024
TOKEN_TILE = 1024
GLA_CHUNK = 64
GLA_TILE = 256
NA_QROWS = 4
NA_KROWS = 12
HALO = 16


def _cparams(n_axes):
    return pltpu.CompilerParams(dimension_semantics=("arbitrary",) * n_axes,
                                vmem_limit_bytes=VMEM_LIMIT)


def _resident(shape):
    return pl.BlockSpec(shape, lambda i: (0,) * len(shape), pipeline_mode=pl.Buffered(1))


def _sigmoid(x):
    return 1.0 / (1.0 + jnp.exp(-x))


def _silu(x):
    return x * _sigmoid(x)


def _log_sigmoid(x):
    return jnp.minimum(x, 0.0) - jnp.log(1.0 + jnp.exp(-jnp.abs(x)))


def _dot(a, b):
    return jnp.dot(a, b, preferred_element_type=F32)


def _dot_nt(a, b):
    return lax.dot_general(a, b, (((1,), (1,)), ((), ())), preferred_element_type=F32)


def _dot_tn(a, b):
    return lax.dot_general(a, b, (((0,), (0,)), ((), ())), preferred_element_type=F32)


def _ada_kernel(cond_ref, w_ref, b_ref, o_ref):
    c = cond_ref[...]
    o_ref[0] = _dot(_silu(c).astype(BF16), w_ref[0].astype(BF16)) + b_ref[0]


def _adaln(cond, w_ada, b_ada):
    r = cond.shape[0]
    tn = 768
    return pl.pallas_call(
        _ada_kernel,
        grid=(DEPTH, 3 * D_MODEL // tn),
        in_specs=[
            pl.BlockSpec((r, D_MODEL), lambda l, n: (0, 0)),
            pl.BlockSpec((1, D_MODEL, tn), lambda l, n: (l, 0, n)),
            pl.BlockSpec((1, 1, tn), lambda l, n: (l, 0, n)),
        ],
        out_specs=pl.BlockSpec((1, r, tn), lambda l, n: (l, 0, n)),
        out_shape=jax.ShapeDtypeStruct((DEPTH, r, 3 * D_MODEL), F32),
        compiler_params=_cparams(2),
        name="adaln",
    )(cond, w_ada, b_ada.reshape(DEPTH, 1, 3 * D_MODEL))


def _modulated_norm(x, mod_ref, nw_ref):
    ms = jnp.mean(x * x, axis=-1, keepdims=True)
    xn = x * lax.rsqrt(ms + EPS) * nw_ref[...]
    shift = mod_ref[0, :, 0:D_MODEL]
    scale = mod_ref[0, :, D_MODEL:2 * D_MODEL]
    return xn * (1.0 + scale) + shift


def _inproj_kernel(*refs, rope):
    if rope:
        (x_ref, mod_ref, nw_ref, w_ref, hm_ref, qw_ref, kw_ref, rc_ref, rs1_ref, rs2_ref,
         pc_ref, gc_ref, q_ref, k_ref, v_ref, gn_ref, qg_ref, kg_ref, vg_ref, gg_ref, lr_ref) = refs
    else:
        (x_ref, mod_ref, nw_ref, w_ref, hm_ref, qw_ref, kw_ref,
         pc_ref, gc_ref, q_ref, k_ref, v_ref, gn_ref, qg_ref, kg_ref, vg_ref, gg_ref, lr_ref) = refs
    h = _modulated_norm(x_ref[...], mod_ref, nw_ref).astype(BF16)

    def proj(c0, n):
        return _dot(h, w_ref[:, c0:c0 + n])

    pc_ref[...] = (proj(C_CA, BRANCH_W) * proj(C_XA, BRANCH_W)).astype(pc_ref.dtype)
    gc_ref[...] = (proj(C_BA, BRANCH_W) * _silu(proj(C_GA, BRANCH_W))).astype(gc_ref.dtype)

    def head_norm(u, g_ref):
        ms = _dot((u * u).astype(BF16), hm_ref[...])
        return u * lax.rsqrt(ms + EPS) * g_ref[...]

    q_ref[...] = (head_norm(proj(C_QN, BRANCH_W), qw_ref) * (NA_HD ** -0.5 * LOG2E)).astype(q_ref.dtype)
    k_ref[...] = head_norm(proj(C_KN, BRANCH_W), kw_ref).astype(k_ref.dtype)
    v_ref[...] = proj(C_VN, BRANCH_W).astype(v_ref.dtype)
    gn_ref[...] = _silu(proj(C_GN, BRANCH_W)).astype(gn_ref.dtype)

    qg = proj(C_QG, GLA_KW) * (GLA_DK ** -0.5)
    kg = proj(C_KG, GLA_KW)
    if rope:
        def rot(u):
            parts = []
            for j in range(GLA_KW // LANES):
                uj = u[:, j * LANES:(j + 1) * LANES]
                parts.append(uj * rc_ref[...]
                             + pltpu.roll(uj, LANES - GLA_DK // 4, axis=1) * rs1_ref[...]
                             + pltpu.roll(uj, GLA_DK // 4, axis=1) * rs2_ref[...])
            return jnp.concatenate(parts, axis=1)
        qg = rot(qg)
        kg = rot(kg)
    qg_ref[...] = qg.astype(qg_ref.dtype)
    kg_ref[...] = kg.astype(kg_ref.dtype)
    vg_ref[...] = proj(C_VG, BRANCH_W).astype(vg_ref.dtype)
    gg_ref[...] = _silu(proj(C_GG, BRANCH_W)).astype(gg_ref.dtype)
    lr_ref[...] = proj(C_LR, LANES)


def _inproj(x, mod, nw, w_in, hm, qw, kw, rope_tabs, seq, kv_dtype):
    n_tok = x.shape[0]
    tm = min(TOKEN_TILE, seq)
    tiles_per_seq = seq // tm
    rope = rope_tabs is not None

    def tok(w):
        return pl.BlockSpec((tm, w), lambda i: (i, 0))

    def const(shape):
        return pl.BlockSpec(shape, lambda i: (0,) * len(shape))

    in_specs = [
        tok(D_MODEL),
        pl.BlockSpec((1, 1, 3 * D_MODEL), lambda i: (i // tiles_per_seq, 0, 0)),
        const((1, D_MODEL)),
        _resident((D_MODEL, D_IN_PAD)),
        const((BRANCH_W, BRANCH_W)),
        const((1, BRANCH_W)),
        const((1, BRANCH_W)),
    ]
    args = [x, mod, nw, w_in, hm, qw, kw]
    if rope:
        in_specs += [pl.BlockSpec((tm, LANES), lambda i: (i % tiles_per_seq, 0))] * 3
        args += list(rope_tabs)
    widths = [BRANCH_W, BRANCH_W, BRANCH_W, BRANCH_W, BRANCH_W, BRANCH_W,
              GLA_KW, GLA_KW, BRANCH_W, BRANCH_W, LANES]
    dtypes = [BF16, BF16, BF16, kv_dtype, kv_dtype, BF16, BF16, BF16, BF16, BF16, F32]
    return pl.pallas_call(
        functools.partial(_inproj_kernel, rope=rope),
        grid=(n_tok // tm,),
        in_specs=in_specs,
        out_specs=[tok(w) for w in widths],
        out_shape=[jax.ShapeDtypeStruct((n_tok, w), d) for w, d in zip(widths, dtypes)],
        compiler_params=_cparams(1),
        name="inproj_rope" if rope else "inproj",
    )(*args)


def _pair_masks(dtype):
    lane = lax.broadcasted_iota(jnp.int32, (1, LANES), 1)
    return lane < NA_HD, lane >= NA_HD


def _softmax_pv(scores, values, head_lanes):
    m = functools.reduce(jnp.maximum, [jnp.max(s, axis=-1, keepdims=True) for s in scores])
    o = None
    for s, v in zip(scores, values):
        p = jnp.exp2(s - m).astype(BF16)
        t = _dot(p, jnp.where(head_lanes, v, jnp.ones_like(v)))
        o = t if o is None else o + t
    return o / pltpu.roll(o, NA_HD, axis=1)


def _na_ctx_kernel(q_ref, k_ref, v_ref, gn_ref, o_ref):
    m0, m1 = _pair_masks(BF16)
    for hp in range(NA_HEADS // 2):
        cols = slice(hp * LANES, (hp + 1) * LANES)
        qp = q_ref[0, :, cols]
        kp = k_ref[0, :, cols].astype(BF16)
        vp = v_ref[0, :, cols].astype(BF16)
        outs = []
        for msk in (m0, m1):
            qa = jnp.where(msk, qp, jnp.zeros_like(qp))
            outs.append(_softmax_pv([_dot_nt(qa, kp)], [vp], msk))
        o = jnp.where(m0, outs[0], outs[1])
        o_ref[0, :, cols] = (o * gn_ref[0, :, cols].astype(F32)).astype(o_ref.dtype)


def _na_ctx(q, k, v, gn, batch, seq):
    def blk():
        return pl.BlockSpec((1, seq, BRANCH_W), lambda b: (b, 0, 0))
    r3 = lambda u: u.reshape(batch, seq, BRANCH_W)
    out = pl.pallas_call(
        _na_ctx_kernel,
        grid=(batch,),
        in_specs=[blk(), blk(), blk(), blk()],
        out_specs=blk(),
        out_shape=jax.ShapeDtypeStruct((batch, seq, BRANCH_W), BF16),
        compiler_params=_cparams(1),
        name="na_ctx",
    )(r3(q), r3(k), r3(v), r3(gn))
    return out.reshape(batch * seq, BRANCH_W)


def _na_window_start(j, rows):
    return jnp.clip(j * NA_QROWS - WIN_R // 2, 0, rows - NA_KROWS)


def _na_lat_kernel(q_ref, k_ref, v_ref, kc_ref, vc_ref, bias_ref, gn_ref, o_ref, *, rows):
    j = pl.program_id(1)
    start = pl.multiple_of(_na_window_start(j, rows) * GRID_W, NA_QROWS * GRID_W)
    m0, m1 = _pair_masks(BF16)
    for hp in range(NA_HEADS // 2):
        cols = slice(hp * LANES, (hp + 1) * LANES)
        qp = q_ref[0, :, cols]
        kw = k_ref[0, pl.ds(start, NA_KROWS * GRID_W), cols]
        vw = v_ref[0, pl.ds(start, NA_KROWS * GRID_W), cols]
        kc = kc_ref[0, :, cols]
        vc = vc_ref[0, :, cols]
        tq = qp.shape[0]
        q2 = jnp.concatenate([jnp.where(m0, qp, jnp.zeros_like(qp)),
                              jnp.where(m1, qp, jnp.zeros_like(qp))], axis=0)
        bias = bias_ref[0, 2 * hp:2 * hp + 2].reshape(2 * tq, NA_KROWS * GRID_W)
        s_w = _dot_nt(q2, kw) + bias.astype(F32)
        s_c = _dot_nt(q2, kc)
        outs = []
        for a, msk in enumerate((m0, m1)):
            rows_a = slice(a * tq, (a + 1) * tq)
            outs.append(_softmax_pv([s_w[rows_a], s_c[rows_a]], [vw, vc], msk))
        o = jnp.where(m0, outs[0], outs[1])
        o_ref[0, :, cols] = (o * gn_ref[0, :, cols].astype(F32)).astype(o_ref.dtype)


def _na_lat(q, k, v, kc, vc, bias, gn, batch, seq):
    rows = seq // GRID_W
    nj = rows // NA_QROWS
    tq = NA_QROWS * GRID_W

    def bias_type(j):
        return jnp.where(j == 0, 0, jnp.where(j == nj - 1, 2, 1))

    qblk = pl.BlockSpec((1, tq, BRANCH_W), lambda b, j: (b, j, 0))
    seqblk = pl.BlockSpec((1, seq, BRANCH_W), lambda b, j: (b, 0, 0))
    ctxblk = pl.BlockSpec((1, kc.shape[1], BRANCH_W), lambda b, j: (b, 0, 0))
    r3 = lambda u: u.reshape(batch, seq, BRANCH_W)
    out = pl.pallas_call(
        functools.partial(_na_lat_kernel, rows=rows),
        grid=(batch, nj),
        in_specs=[qblk, seqblk, seqblk, ctxblk, ctxblk,
                  pl.BlockSpec((1, NA_HEADS, tq, NA_KROWS * GRID_W),
                               lambda b, j: (bias_type(j), 0, 0, 0)),
                  qblk],
        out_specs=qblk,
        out_shape=jax.ShapeDtypeStruct((batch, seq, BRANCH_W), BF16),
        compiler_params=_cparams(2),
        name="na_lat",
    )(r3(q), r3(k), r3(v), kc, vc, bias, r3(gn))
    return out.reshape(batch * seq, BRANCH_W)


def _na_bias_kernel(rpb_ref, o_ref, *, rows):
    nj = rows // NA_QROWS
    qc = lax.broadcasted_iota(jnp.int32, (GRID_W, LANES), 0)
    lane = lax.broadcasted_iota(jnp.int32, (GRID_W, LANES), 1)
    kc = lane & (GRID_W - 1)
    cs = jnp.clip(qc - WIN_C // 2, 0, GRID_W - WIN_C)
    col_ok = (kc >= cs) & (kc < cs + WIN_C)
    left = lane < GRID_W
    neg = jnp.full((GRID_W, LANES), NEG_INF, F32)
    tl, tr = [], []
    for d in range(2 * WIN_R - 1):
        x = jnp.broadcast_to(rpb_ref[0, d:d + 1, :] * LOG2E, (GRID_W, LANES))
        tl.append(pltpu.roll(x, LANES - (WIN_C - 1), 1, stride=1, stride_axis=0))
        tr.append(pltpu.roll(x, GRID_W - (WIN_C - 1), 1, stride=1, stride_axis=0))
    for t, j in enumerate((0, 1, nj - 1)):
        start = int(np.clip(j * NA_QROWS - WIN_R // 2, 0, rows - NA_KROWS))
        for rl in range(NA_QROWS):
            r = j * NA_QROWS + rl
            rs = int(np.clip(r - WIN_R // 2, 0, rows - WIN_R))
            for kp in range(NA_KROWS // 2):
                k0 = start + 2 * kp
                a = tl[k0 - r + WIN_R - 1] if rs <= k0 < rs + WIN_R else neg
                b = tr[k0 + 1 - r + WIN_R - 1] if rs <= k0 + 1 < rs + WIN_R else neg
                blk = jnp.where(col_ok, jnp.where(left, a, b), NEG_INF)
                o_ref[t, 0, rl * GRID_W:(rl + 1) * GRID_W, kp * LANES:(kp + 1) * LANES] = (
                    blk.astype(o_ref.dtype))


def _na_bias_table(rpb, rows):
    rp = jnp.pad(rpb, ((0, 0), (0, 0), (0, LANES - (2 * WIN_C - 1))))
    tq, tk = NA_QROWS * GRID_W, NA_KROWS * GRID_W
    return pl.pallas_call(
        functools.partial(_na_bias_kernel, rows=rows),
        grid=(NA_HEADS,),
        in_specs=[pl.BlockSpec((1, 2 * WIN_R - 1, LANES), lambda h: (h, 0, 0))],
        out_specs=pl.BlockSpec((3, 1, tq, tk), lambda h: (0, h, 0, 0)),
        out_shape=jax.ShapeDtypeStruct((3, NA_HEADS, tq, tk), BF16),
        compiler_params=_cparams(1),
        name="na_bias",
    )(rp)


def _gla_kernel(q_ref, k_ref, v_ref, gg_ref, lr_ref, wa_ref, ba_ref, gw_ref, bd_ref, pm_ref,
                s0f_ref, s0b_ref, y_ref, sf_ref, sb_ref,
                of_ref, ob_ref, stf_ref, stb_ref, qi_ref, ku_ref, dec_ref, *, seq):
    c = GLA_CHUNK
    n = seq // c
    t = GLA_TILE
    lane = lax.broadcasted_iota(jnp.int32, (1, GLA_KW), 1)
    head_masks = [(lane >= h * GLA_DK) & (lane < (h + 1) * GLA_DK) for h in range(GLA_HEADS)]
    row = lax.broadcasted_iota(jnp.int32, (t, t), 0)
    col = lax.broadcasted_iota(jnp.int32, (t, t), 1)
    same_chunk = (row // c) == (col // c)
    causal = (same_chunk & (row >= col), same_chunk & (row <= col))
    wa = wa_ref[...].astype(BF16)
    outs = (of_ref, ob_ref)

    def chunk_rows(x, r):
        return jnp.concatenate(
            [jnp.broadcast_to(x[j * c + r:j * c + r + 1], (c, GLA_KW)) for j in range(t // c)], axis=0)

    def prep(i, carry):
        sl = pl.ds(pl.multiple_of(i * t, t), t)
        lr = lr_ref[0, sl, :].astype(BF16)
        q = q_ref[0, sl, :].astype(F32)
        k = k_ref[0, sl, :].astype(F32)
        v = v_ref[0, sl, :]
        for d in range(2):
            cols = slice(d * GLA_KW, (d + 1) * GLA_KW)
            la = _log_sigmoid(_dot(lr, wa[:, cols]) + ba_ref[:, cols]) * (1.0 / GLA_TAU)
            la_hi = la.astype(BF16)
            la_lo = (la - la_hi.astype(F32)).astype(BF16)
            g_cum = _dot(pm_ref[d], la_hi) + _dot(pm_ref[d], la_lo)
            g_all = chunk_rows(g_cum, (c - 1, 0)[d])
            g_ref = g_cum - chunk_rows(g_cum, (c // 2 - 1, c // 2)[d])
            qd = (q * jnp.exp(g_ref)).astype(BF16)
            kd = (k * jnp.exp(-g_ref)).astype(BF16)
            qi_ref[d, sl, :] = (q * jnp.exp(g_cum)).astype(BF16)
            ku_ref[d, sl, :] = (k * jnp.exp(g_all - g_cum)).astype(BF16)
            dec_ref[d, sl, :] = jnp.exp(g_all)
            parts = []
            for h in range(GLA_HEADS):
                qh = jnp.where(head_masks[h], qd, jnp.zeros_like(qd))
                att = jnp.where(causal[d], _dot_nt(qh, kd), 0.0)
                parts.append(_dot(att.astype(BF16), v[:, h * GLA_DV:(h + 1) * GLA_DV]))
            outs[d][sl, :] = jnp.concatenate(parts, axis=1)
        return carry

    lax.fori_loop(0, seq // t, prep, 0, unroll=4)

    stf_ref[...] = s0f_ref[0]
    stb_ref[...] = s0b_ref[0]

    def step(d, r0, st_ref):
        sl = pl.ds(r0, c)
        st = st_ref[...]
        outs[d][sl, :] += _dot_nt(qi_ref[d, sl, :], st.astype(BF16))
        upd = _dot_tn(v_ref[0, sl, :], ku_ref[d, sl, :])
        st_ref[...] = st * dec_ref[d, pl.ds(r0, 1), :] + upd * bd_ref[...]

    def scan(i, carry):
        step(0, pl.multiple_of(i * c, c), stf_ref)
        step(1, pl.multiple_of((n - 1 - i) * c, c), stb_ref)
        return carry

    lax.fori_loop(0, n, scan, 0, unroll=8)
    sf_ref[0] = stf_ref[...]
    sb_ref[0] = stb_ref[...]

    def finish(i, carry):
        sl = pl.ds(pl.multiple_of(i * t, t), t)
        o = of_ref[sl, :] + ob_ref[sl, :]
        parts = []
        for h in range(GLA_HEADS):
            oh = o[:, h * GLA_DV:(h + 1) * GLA_DV]
            ms = jnp.mean(oh * oh, axis=-1, keepdims=True)
            parts.append(oh * lax.rsqrt(ms + EPS))
        y = jnp.concatenate(parts, axis=1) * gw_ref[...]
        y_ref[0, sl, :] = (y * gg_ref[0, sl, :].astype(F32)).astype(y_ref.dtype)
        return carry

    lax.fori_loop(0, seq // t, finish, 0)


def _gla_sum_matrices():
    t, c = GLA_TILE, GLA_CHUNK
    r = np.arange(t)[:, None]
    j = np.arange(t)[None, :]
    same = (r // c) == (j // c)
    mats = [(same & (j <= r)).astype(np.float32), (same & (j >= r)).astype(np.float32)]
    return jnp.asarray(np.stack(mats), BF16)


def _gla(qg, kg, vg, gg, lr, wa, ba, gw, bd, pm, s0f, s0b, batch, seq):
    hv, hk = GLA_HEADS * GLA_DV, GLA_KW
    assert seq % GLA_TILE == 0

    def seqblk(w):
        return pl.BlockSpec((1, seq, w), lambda b: (b, 0, 0))

    def const(shape):
        return pl.BlockSpec(shape, lambda b: (0,) * len(shape))

    stblk = pl.BlockSpec((1, hv, hk), lambda b: (b, 0, 0))
    r3 = lambda u: u.reshape(batch, seq, u.shape[-1])
    y, sf, sb = pl.pallas_call(
        functools.partial(_gla_kernel, seq=seq),
        grid=(batch,),
        in_specs=[seqblk(hk), seqblk(hk), seqblk(hv), seqblk(hv), seqblk(LANES),
                  const((LANES, 2 * hk)), const((1, 2 * hk)), const((1, hv)), const((hv, hk)),
                  const((2, GLA_TILE, GLA_TILE)), stblk, stblk],
        out_specs=[seqblk(hv), stblk, stblk],
        out_shape=[jax.ShapeDtypeStruct((batch, seq, hv), BF16),
                   jax.ShapeDtypeStruct((batch, hv, hk), F32),
                   jax.ShapeDtypeStruct((batch, hv, hk), F32)],
        scratch_shapes=[pltpu.VMEM((seq, hv), F32), pltpu.VMEM((seq, hv), F32),
                        pltpu.VMEM((hv, hk), F32), pltpu.VMEM((hv, hk), F32),
                        pltpu.VMEM((2, seq, hk), BF16), pltpu.VMEM((2, seq, hk), BF16),
                        pltpu.VMEM((2, seq, hk), F32)],
        compiler_params=_cparams(1),
        name="gla",
    )(r3(qg), r3(kg), r3(vg), r3(gg), r3(lr), wa, ba, gw, bd, pm, s0f, s0b)
    return y.reshape(batch * seq, hv), sf, sb


def _state_to_blockdiag_t(s):
    b = s.shape[0]
    eye = jnp.eye(GLA_HEADS, dtype=s.dtype)
    t = jnp.einsum("bhkv,hg->bhvgk", s, eye)
    return t.reshape(b, GLA_HEADS * GLA_DV, GLA_KW)


def _blockdiag_t_to_state(t):
    b = t.shape[0]
    t5 = t.reshape(b, GLA_HEADS, GLA_DV, GLA_HEADS, GLA_DK)
    d = jnp.stack([t5[:, h, :, h, :] for h in range(GLA_HEADS)], axis=1)
    return d.transpose(0, 1, 3, 2)


def _merge_kernel(x_ref, mod_ref, nw_ref, pc_ref, pprev_ref, pnext_ref, gc_ref, cw_ref,
                  yna_ref, ygla_ref, wg_ref, bg_ref, wb_ref, wo_ref, o_ref, *, tiles_per_seq):
    i = pl.program_id(0)
    x = x_ref[...]
    h = _modulated_norm(x, mod_ref, nw_ref).astype(BF16)
    tm = x.shape[0]

    pc = pc_ref[...].astype(F32)
    pos = i % tiles_per_seq
    prev_row = jnp.where(pos == 0, 0.0, pprev_ref[HALO - 1:HALO, :].astype(F32))
    next_row = jnp.where(pos == tiles_per_seq - 1, 0.0, pnext_ref[0:1, :].astype(F32))
    ridx = lax.broadcasted_iota(jnp.int32, (tm, 1), 0)
    before = jnp.where(ridx == 0, prev_row, pltpu.roll(pc, 1, axis=0))
    after = jnp.where(ridx == tm - 1, next_row, pltpu.roll(pc, tm - 1, axis=0))
    conv = before * cw_ref[0:1, :] + pc * cw_ref[1:2, :] + after * cw_ref[2:3, :]
    y_conv = (gc_ref[...].astype(F32) * conv).astype(BF16)

    merged = None
    for b, yb in enumerate((y_conv, yna_ref[...], ygla_ref[...])):
        cols = slice(b * D_MODEL, (b + 1) * D_MODEL)
        g = _sigmoid(_dot(h, wg_ref[:, cols]) + bg_ref[:, cols])
        t = g * _dot(yb, wb_ref[b])
        merged = t if merged is None else merged + t
    gate = mod_ref[0, :, 2 * D_MODEL:3 * D_MODEL]
    o_ref[...] = x + gate * _dot(merged.astype(BF16), wo_ref[...])


def _merge(x, mod, nw, pc, gc, cw, yna, ygla, wg, bg, wb, wo, seq):
    n_tok = x.shape[0]
    tm = min(TOKEN_TILE, seq)
    tiles_per_seq = seq // tm
    hb = tm // HALO
    last = n_tok // HALO - 1

    def tok(w):
        return pl.BlockSpec((tm, w), lambda i: (i, 0))

    def const(shape):
        return pl.BlockSpec(shape, lambda i: (0,) * len(shape))

    return pl.pallas_call(
        functools.partial(_merge_kernel, tiles_per_seq=tiles_per_seq),
        grid=(n_tok // tm,),
        in_specs=[
            tok(D_MODEL),
            pl.BlockSpec((1, 1, 3 * D_MODEL), lambda i: (i // tiles_per_seq, 0, 0)),
            const((1, D_MODEL)),
            tok(BRANCH_W),
            pl.BlockSpec((HALO, BRANCH_W), lambda i: (jnp.maximum(i * hb - 1, 0), 0)),
            pl.BlockSpec((HALO, BRANCH_W), lambda i: (jnp.minimum((i + 1) * hb, last), 0)),
            tok(BRANCH_W),
            const((3, BRANCH_W)),
            tok(BRANCH_W),
            tok(BRANCH_W),
            _resident((D_MODEL, N_BRANCH * D_MODEL)),
            const((1, N_BRANCH * D_MODEL)),
            _resident((N_BRANCH, BRANCH_W, D_MODEL)),
            _resident((D_MODEL, D_MODEL)),
        ],
        out_specs=tok(D_MODEL),
        out_shape=jax.ShapeDtypeStruct((n_tok, D_MODEL), F32),
        compiler_params=_cparams(1),
        name="merge",
    )(x, mod, nw, pc, pc, pc, gc, cw, yna, ygla, wg, bg, wb, wo)


def _rope_tables(seq):
    pos = np.arange(seq)
    n_f = GLA_DK // 4
    inv = ROPE_BASE ** (-np.arange(n_f) / n_f)
    ang_r = ((pos // GRID_W)[:, None] * inv).astype(np.float32).astype(np.float64)
    ang_c = ((pos % GRID_W)[:, None] * inv).astype(np.float32).astype(np.float64)
    zero = np.zeros_like(ang_r)
    cos = np.concatenate([np.cos(ang_r), np.cos(ang_r), np.cos(ang_c), np.cos(ang_c)], axis=1)
    s1 = np.concatenate([-np.sin(ang_r), zero, -np.sin(ang_c), zero], axis=1)
    s2 = np.concatenate([zero, np.sin(ang_r), zero, np.sin(ang_c)], axis=1)
    tile = lambda t: jnp.asarray(np.tile(t, (1, LANES // GLA_DK)).astype(np.float32))
    return tile(cos), tile(s1), tile(s2)


def _head_mean_matrix():
    h = np.arange(BRANCH_W) // NA_HD
    return jnp.asarray((h[:, None] == h[None, :]).astype(np.float32) / NA_HD, BF16)


def _gla_blockdiag_mask():
    r = np.arange(GLA_HEADS * GLA_DV) // GLA_DV
    c = np.arange(GLA_KW) // GLA_DK
    return jnp.asarray((r[:, None] == c[None, :]).astype(np.float32), F32)


def kernel(x_prompt, x_sample, c, cache_k, cache_v, state_fwd, state_bwd, c_ctx,
           norm_w, w_ada, b_ada, w_in, conv_w, q_norm_w, k_norm_w, rpb,
           w_alpha, b_alpha, gla_norm_w, w_branch, w_gate, b_gate, w_out):
    nb_p, seq_p, _ = x_prompt.shape
    nb_s, seq_s, _ = x_sample.shape
    past = cache_k.shape[3]

    cond = jnp.concatenate([c, c_ctx[None, :], jnp.zeros((16 - nb_s - 1, D_MODEL), F32)], axis=0)
    mods = _adaln(cond, w_ada, b_ada)

    w_in_b = jnp.pad(w_in, ((0, 0), (0, 0), (0, D_IN_PAD - D_IN))).astype(BF16)
    w_gate_b = w_gate.astype(BF16)
    w_branch_b = w_branch.astype(BF16)
    w_out_b = w_out.astype(BF16)
    hm = _head_mean_matrix()
    bd = _gla_blockdiag_mask()
    pm = _gla_sum_matrices()
    rope_tabs = _rope_tables(seq_s)

    ck = cache_k.transpose(0, 1, 3, 2, 4).reshape(nb_s, DEPTH, past, BRANCH_W).astype(BF16)
    cv = cache_v.transpose(0, 1, 3, 2, 4).reshape(nb_s, DEPTH, past, BRANCH_W).astype(BF16)

    y_p = x_prompt.reshape(nb_p * seq_p, D_MODEL)
    y_s = x_sample.reshape(nb_s * seq_s, D_MODEL)
    zeros_state = jnp.zeros((nb_p, GLA_HEADS * GLA_DV, GLA_KW), F32)
    ks, vs, sfs, sbs = [], [], [], []
    for l in range(DEPTH):
        nw = norm_w[l].reshape(1, D_MODEL)
        qw = jnp.tile(q_norm_w[l], NA_HEADS).reshape(1, BRANCH_W)
        kw = jnp.tile(k_norm_w[l], NA_HEADS).reshape(1, BRANCH_W)
        gw = jnp.tile(gla_norm_w[l], GLA_HEADS).reshape(1, BRANCH_W)
        wa = jnp.zeros((LANES, 2 * GLA_KW), F32)
        wa = wa.at[0:GLA_RANK, 0:GLA_KW].set(w_alpha[l, 0])
        wa = wa.at[GLA_RANK:2 * GLA_RANK, GLA_KW:].set(w_alpha[l, 1])
        ba = b_alpha[l].reshape(1, 2 * GLA_KW)
        bg = b_gate[l].reshape(1, N_BRANCH * D_MODEL)

        mod_p = mods[l, nb_s:nb_s + 1].reshape(1, 1, 3 * D_MODEL)
        mod_p = jnp.broadcast_to(mod_p, (nb_p, 1, 3 * D_MODEL))
        (pc, gc, q, k, v, gn, qg, kg, vg, gg, lr) = _inproj(
            y_p, mod_p, nw, w_in_b[l], hm, qw, kw, None, seq_p, F32)
        yna = _na_ctx(q, k, v, gn, nb_p, seq_p)
        ygla, sf, sb = _gla(qg, kg, vg, gg, lr, wa, ba, gw, bd, pm, zeros_state, zeros_state,
                            nb_p, seq_p)
        y_p = _merge(y_p, mod_p, nw, pc, gc, conv_w[l], yna, ygla,
                     w_gate_b[l], bg, w_branch_b[l], w_out_b[l], seq_p)
        ks.append(k.reshape(nb_p, seq_p, NA_HEADS, NA_HD).transpose(0, 2, 1, 3))
        vs.append(v.reshape(nb_p, seq_p, NA_HEADS, NA_HD).transpose(0, 2, 1, 3))
        sfs.append(_blockdiag_t_to_state(sf))
        sbs.append(_blockdiag_t_to_state(sb))

        mod_s = mods[l, 0:nb_s].reshape(nb_s, 1, 3 * D_MODEL)
        (pc, gc, q, k, v, gn, qg, kg, vg, gg, lr) = _inproj(
            y_s, mod_s, nw, w_in_b[l], hm, qw, kw, rope_tabs, seq_s, BF16)
        bias = _na_bias_table(rpb[l], seq_s // GRID_W)
        yna = _na_lat(q, k, v, ck[:, l], cv[:, l], bias, gn, nb_s, seq_s)
        ygla, _, _ = _gla(qg, kg, vg, gg, lr, wa, ba, gw, bd, pm,
                          _state_to_blockdiag_t(state_fwd[:, l]),
                          _state_to_blockdiag_t(state_bwd[:, l]), nb_s, seq_s)
        y_s = _merge(y_s, mod_s, nw, pc, gc, conv_w[l], yna, ygla,
                     w_gate_b[l], bg, w_branch_b[l], w_out_b[l], seq_s)

    return (y_p.reshape(nb_p, seq_p, D_MODEL), y_s.reshape(nb_s, seq_s, D_MODEL),
            jnp.stack(ks, axis=1), jnp.stack(vs, axis=1),
            jnp.stack(sfs, axis=1), jnp.stack(sbs, axis=1))
```
